```python
import jax, jax.numpy as jnp
from jax import lax
import numpy as np

D_MODEL = 1024
BATCH = 4
SEQ = 8192
DEPTH = 4

N_MIXERS = 3
A_DILATIONS = (1, 4, 16)
A_WINDOWS = (128, 512, 2048)
A_GROUPS = 3
A_HEADS_PER_GROUP = 8
A_HEAD_DIM = 64
A_GROUP_WIDTH = A_HEADS_PER_GROUP * A_HEAD_DIM
A_STEPS = 128
A_BLOCK = 128
A_HEADS_TOTAL = A_GROUPS * A_HEADS_PER_GROUP
REL_BUCKETS = 32
REL_MAX_EXACT = REL_BUCKETS // 2
REL_MAX_DIST = 2048
B_WINDOWS = (2, 4, 8, 16)
B_GROUPS = 4
B_GROUP_WIDTH = D_MODEL // B_GROUPS
C_CHUNK = 128
C_GROUPS = 4
C_GROUP_WIDTH = D_MODEL // C_GROUPS
FFN_HIDDEN = -(-8 * D_MODEL // (3 * 256)) * 256
PLE_DIM = 256
N_A = (DEPTH + 2) // 3
N_B = (DEPTH + 1) // 3
N_C = DEPTH // 3
EPS = 1e-6
NEG = -1e30

kernel_name = "hybrid_dilated_pool_sgu_trunk"


def rmsnorm(x, g):
    xf = x.astype(jnp.float32)
    y = xf * lax.rsqrt(jnp.mean(xf * xf, axis=-1, keepdims=True) + EPS)
    return (y * g.astype(jnp.float32)).astype(x.dtype)


def t5_bucket(n):
    nf = np.maximum(n, 1).astype(np.float32)
    large = REL_MAX_EXACT + (np.log(nf / REL_MAX_EXACT) / np.log(REL_MAX_DIST / REL_MAX_EXACT)
                             * (REL_BUCKETS - REL_MAX_EXACT)).astype(np.int32)
    large = np.minimum(large, REL_BUCKETS - 1)
    return np.where(n < REL_MAX_EXACT, n, large).astype(np.int32)


def rel_bias_block(table_g, d):
    i = np.arange(A_BLOCK)[:, None]
    j = np.arange(2 * A_BLOCK)[None, :]
    rel = A_BLOCK + i - j
    valid = (rel >= 0) & (rel <= A_STEPS)
    bucket = t5_bucket(np.where(valid, rel, 0) * d)
    bias = jnp.transpose(table_g[bucket], (2, 0, 1))
    return bias, valid


def banded_branch(q, k, v, bias, valid):
    N, L, H, Dh = q.shape
    nb = -(-L // A_BLOCK)
    Lp = nb * A_BLOCK
    pad = ((0, 0), (0, Lp - L), (0, 0), (0, 0))
    qb, kb, vb = [jnp.pad(t, pad).reshape(N, nb, A_BLOCK, H, Dh) for t in (q, k, v)]

    def with_prev(t):
        prev = jnp.pad(t, ((0, 0), (1, 0), (0, 0), (0, 0), (0, 0)))[:, :-1]
        return jnp.concatenate([prev, t], axis=2)

    kk, vv = with_prev(kb), with_prev(vb)
    s = jnp.einsum('nbqhd,nbkhd->nhbqk', qb, kk).astype(jnp.float32) * (Dh ** -0.5)
    s = s + bias[:, None].astype(jnp.float32)
    key_pos = (np.arange(nb)[:, None, None] * A_BLOCK + np.arange(2 * A_BLOCK)[None, None, :] - A_BLOCK)
    mask = valid[None] & (key_pos >= 0)
    s = jnp.where(mask, s, NEG)
    m = jnp.max(s, axis=-1, keepdims=True)
    e = jnp.exp(s - m)
    den = jnp.sum(e, axis=-1, keepdims=True)
    o = jnp.einsum('nhbqk,nbkhd->nbqhd', (e / den).astype(v.dtype), vv)
    lse = (m + jnp.log(den))[..., 0]
    o = o.reshape(N, Lp, H, Dh)[:, :L]
    lse = jnp.transpose(lse, (0, 2, 3, 1)).reshape(N, Lp, H)[:, :L]
    return o, lse


def mixer_dilated(h, w_qkv, w_o, rel_table):
    B, S, _ = h.shape
    H, Dh = A_HEADS_PER_GROUP, A_HEAD_DIM
    qkv = (h @ w_qkv).reshape(B, S, 3, A_GROUPS, H, Dh)
    outs, lses = [], []
    for g, d in enumerate(A_DILATIONS):
        L = S // d

        def to_phase(t):
            return t.reshape(B, L, d, H, Dh).transpose(0, 2, 1, 3, 4).reshape(B * d, L, H, Dh)

        q, k, v = (to_phase(qkv[:, :, c, g]) for c in range(3))
        bias, valid = rel_bias_block(rel_table[:, g * H:(g + 1) * H], d)
        o, lse = banded_branch(q, k, v, bias, valid)
        outs.append(o.reshape(B, d, L, H, Dh).transpose(0, 2, 1, 3, 4).reshape(B, S, H, Dh))
        lses.append(lse.reshape(B, d, L, H).transpose(0, 2, 1, 3).reshape(B, S, H))
    w = jax.nn.softmax(jnp.stack(lses), axis=0)
    o = jnp.einsum('gbsh,gbshd->bshd', w, jnp.stack(outs).astype(jnp.float32))
    return o.reshape(B, S, A_GROUP_WIDTH).astype(h.dtype) @ w_o


def mixer_pool(h, w_in, w_grp, scale, w_out):
    B, S, _ = h.shape
    y = (h @ w_in).reshape(B, S, B_GROUPS, B_GROUP_WIDTH)
    pos = jnp.arange(1, S + 1, dtype=jnp.float32)[:, None]
    pooled = []
    for g, win in enumerate(B_WINDOWS):
        yg = y[:, :, g].astype(jnp.float32)
        c = jnp.cumsum(yg, axis=1)
        c_prev = jnp.pad(c, ((0, 0), (win, 0), (0, 0)))[:, :S]
        mean = (c - c_prev) / jnp.minimum(pos, win)
        pooled.append(mean - yg)
    z = jnp.stack(pooled, axis=2).astype(h.dtype)
    z = jnp.einsum('bsgc,gce->bsge', z, w_grp).reshape(B, S, D_MODEL) * scale
    return z @ w_out


def mixer_sgu(h, w_in, v_gain, w_s, b_s, w_out):
    B, S, _ = h.shape
    z = jax.nn.gelu(h @ w_in)
    u, v = jnp.split(z, 2, axis=-1)
    vf = v.astype(jnp.float32)
    mu = jnp.mean(vf, axis=-1, keepdims=True)
    var = jnp.mean(jnp.square(vf - mu), axis=-1, keepdims=True)
    vn = ((vf - mu) * lax.rsqrt(var + EPS) * v_gain.astype(jnp.float32)).astype(h.dtype)
    vn = vn.reshape(B, S // C_CHUNK, C_CHUNK, C_GROUPS, C_GROUP_WIDTH)
    wm = w_s * jnp.tril(jnp.ones((C_CHUNK, C_CHUNK), w_s.dtype))
    sp = jnp.einsum('gts,bnsgc->bntgc', wm, vn) + jnp.transpose(b_s)[:, :, None]
    out = u * sp.reshape(B, S, D_MODEL)
    return out @ w_out


def swiglu(h, w_gate, w_up, w_down):
    return (jax.nn.silu(h @ w_gate) * (h @ w_up)) @ w_down


def setup_inputs(seed: int = 0) -> dict:
    key = jax.random.key(seed)
    ks = jax.random.split(key, 24)
    f32 = jnp.float32

    def w(k, shape, fan_in):
        return jax.random.normal(k, shape, f32) * (fan_in ** -0.5)

    def gain(k, shape):
        return 1.0 + 0.1 * jax.random.normal(k, shape, f32)

    return {
        'x': jax.random.normal(ks[0], (BATCH, SEQ, D_MODEL), f32),
        'p': jax.random.normal(ks[1], (DEPTH, BATCH, SEQ, PLE_DIM), f32),
        'rel_table': 0.3 * jax.random.normal(ks[2], (REL_BUCKETS, A_HEADS_TOTAL), f32),
        'norm_mix': gain(ks[3], (DEPTH, D_MODEL)),
        'norm_ffn': gain(ks[4], (DEPTH, D_MODEL)),
        'norm_ple': gain(ks[5], (DEPTH, D_MODEL)),
        'final_norm': gain(ks[6], (D_MODEL,)),
        'a_w_qkv': w(ks[7], (N_A, D_MODEL, 3 * A_GROUPS * A_GROUP_WIDTH), D_MODEL),
        'a_w_o': w(ks[8], (N_A, A_GROUP_WIDTH, D_MODEL), A_GROUP_WIDTH),
        'b_w_in': w(ks[9], (N_B, D_MODEL, D_MODEL), D_MODEL),
        'b_w_grp': w(ks[10], (N_B, B_GROUPS, B_GROUP_WIDTH, B_GROUP_WIDTH), B_GROUP_WIDTH),
        'b_scale': gain(ks[11], (N_B, D_MODEL)),
        'b_w_out': w(ks[12], (N_B, D_MODEL, D_MODEL), D_MODEL),
        'c_w_in': w(ks[13], (N_C, D_MODEL, 2 * D_MODEL), D_MODEL),
        'c_v_gain': gain(ks[14], (N_C, D_MODEL)),
        'c_w_s': w(ks[15], (N_C, C_GROUPS, C_CHUNK, C_CHUNK), C_CHUNK),
        'c_b_s': gain(ks[16], (N_C, C_GROUPS, C_CHUNK)),
        'c_w_out': w(ks[17], (N_C, D_MODEL, D_MODEL), D_MODEL),
        'ffn_w_gate': w(ks[18], (DEPTH, D_MODEL, FFN_HIDDEN), D_MODEL),
        'ffn_w_up': w(ks[19], (DEPTH, D_MODEL, FFN_HIDDEN), D_MODEL),
        'ffn_w_down': w(ks[20], (DEPTH, FFN_HIDDEN, D_MODEL), FFN_HIDDEN),
        'ple_w_gate': w(ks[21], (DEPTH, D_MODEL, D_MODEL), D_MODEL),
        'ple_w_proj': w(ks[22], (DEPTH, PLE_DIM, D_MODEL), PLE_DIM),
    }


def reference(x, p, rel_table, norm_mix, norm_ffn, norm_ple, final_norm,
              a_w_qkv, a_w_o, b_w_in, b_w_grp, b_scale, b_w_out,
              c_w_in, c_v_gain, c_w_s, c_b_s, c_w_out,
              ffn_w_gate, ffn_w_up, ffn_w_down, ple_w_gate, ple_w_proj):
    h = x
    for i in range(DEPTH):
        hn = rmsnorm(h, norm_mix[i])
        kind, j = i % N_MIXERS, i // N_MIXERS
        if kind == 0:
            y = mixer_dilated(hn, a_w_qkv[j], a_w_o[j], rel_table)
        elif kind == 1:
            y = mixer_pool(hn, b_w_in[j], b_w_grp[j], b_scale[j], b_w_out[j])
        else:
            y = mixer_sgu(hn, c_w_in[j], c_v_gain[j], c_w_s[j], c_b_s[j], c_w_out[j])
        h = h + y
        h = h + swiglu(rmsnorm(h, norm_ffn[i]), ffn_w_gate[i], ffn_w_up[i], ffn_w_down[i])
        gate = jax.nn.sigmoid(rmsnorm(h, norm_ple[i]) @ ple_w_gate[i])
        h = h + (p[i] @ ple_w_proj[i]) * gate
    return rmsnorm(h, final_norm)
```

```python
import functools

import jax
import jax.numpy as jnp
import numpy as np
from jax import lax
from jax.experimental import pallas as pl
from jax.experimental.pallas import tpu as pltpu

EPS = 1e-6
NEG = -1e30

A_DILATIONS = (1, 4, 16)
A_GROUPS = 3
A_HEADS = 8
A_HEAD_DIM = 64
A_GROUP_WIDTH = A_HEADS * A_HEAD_DIM
A_BLOCK = 128
REL_BUCKETS = 32
REL_MAX_EXACT = REL_BUCKETS // 2
REL_MAX_DIST = 2048
B_WINDOWS = (2, 4, 8, 16)
B_HALO = 16
C_CHUNK = 128
C_GROUPS = 4

LANES = 128
V7X_VMEM_LIMIT_BYTES = 56 * 1024 * 1024

BF16 = jnp.bfloat16
F32 = jnp.float32


def _dot(a, b):
    return jnp.dot(a, b, preferred_element_type=F32)


def _rms(x, g):
    ms = jnp.mean(x * x, axis=-1, keepdims=True)
    return x * lax.rsqrt(ms + EPS) * g


def _resident(shape):
    nd = len(shape)
    return pl.BlockSpec(shape, lambda *_: (0,) * nd, pipeline_mode=pl.Buffered(1))


def _params(n_grid_dims):
    return pltpu.CompilerParams(
        dimension_semantics=("parallel",) * n_grid_dims,
        vmem_limit_bytes=V7X_VMEM_LIMIT_BYTES,
    )


def _ffn_ple_body(h_ref, p_ref, gf_ref, wgu_ref, wd_ref, gp_ref, wpg_ref, wpp_ref, gl_ref,
                  o_ref, *, n_chunks, th, final_norm):
    h = h_ref[0]
    xn = _rms(h, gf_ref[...]).astype(BF16)
    acc = jnp.zeros_like(h)
    for c in range(n_chunks):
        gu = _dot(xn, wgu_ref[c])
        g = gu[:, :th]
        u = gu[:, th:]
        a = (g * jax.nn.sigmoid(g) * u).astype(BF16)
        acc = acc + _dot(a, wd_ref[c])
    h = h + acc
    gate = jax.nn.sigmoid(_dot(_rms(h, gp_ref[...]).astype(BF16), wpg_ref[...]))
    proj = _dot(p_ref[0].astype(BF16), wpp_ref[...])
    h = h + proj * gate
    if final_norm:
        h = _rms(h, gl_ref[...])
    o_ref[0] = h


def _ffn_ple(h, p, g_ffn, wgu, wd, g_ple, w_pg, w_pp, g_last, *, final_norm, tm):
    B, S, D = h.shape
    P = p.shape[-1]
    n_chunks, _, th2 = wgu.shape
    body = functools.partial(_ffn_ple_body, n_chunks=n_chunks, th=th2 // 2, final_norm=final_norm)
    tile = lambda w: pl.BlockSpec((1, tm, w), lambda b, i: (b, i, 0))
    return pl.pallas_call(
        body,
        grid=(B, S // tm),
        in_specs=[tile(D), tile(P), _resident((1, D)), _resident(wgu.shape), _resident(wd.shape),
                  _resident((1, D)), _resident(w_pg.shape), _resident(w_pp.shape), _resident((1, D))],
        out_specs=tile(D),
        out_shape=jax.ShapeDtypeStruct((B, S, D), F32),
        compiler_params=_params(2),
        name="ffn_ple",
    )(h, p, g_ffn, wgu, wd, g_ple, w_pg, w_pp, g_last)


def _qkv_body(h_ref, g_ref, w_ref, o_ref, *, n_q_cols, tn):
    xn = _rms(h_ref[0], g_ref[...]).astype(BF16)
    n = w_ref.shape[1]
    for c in range(n // tn):
        y = _dot(xn, w_ref[:, c * tn:(c + 1) * tn])
        if (c + 1) * tn <= n_q_cols:
            y = y * (A_HEAD_DIM ** -0.5)
        o_ref[0, :, c * tn:(c + 1) * tn] = y.astype(BF16)


def _qkv_proj(h, g, w, *, tm, tn=512):
    B, S, D = h.shape
    N = w.shape[1]
    body = functools.partial(_qkv_body, n_q_cols=N // 3, tn=tn)
    return pl.pallas_call(
        body,
        grid=(B, S // tm),
        in_specs=[pl.BlockSpec((1, tm, D), lambda b, i: (b, i, 0)), _resident((1, D)), _resident(w.shape)],
        out_specs=pl.BlockSpec((1, tm, N), lambda b, i: (b, i, 0)),
        out_shape=jax.ShapeDtypeStruct((B, S, N), BF16),
        compiler_params=_params(2),
        name="qkv_proj",
    )(h, g, w)


def _attn_body(q_ref, kc_ref, kp_ref, vc_ref, vp_ref, bias_ref, o_ref, lse_ref, kbuf, vbuf, *, tq):
    first = pl.program_id(2) == 0
    kbuf[:A_BLOCK] = kp_ref[0]
    kbuf[A_BLOCK:] = kc_ref[0]
    vbuf[:A_BLOCK] = vp_ref[0]
    vbuf[A_BLOCK:] = vc_ref[0]
    lane = lax.broadcasted_iota(jnp.int32, (1, LANES), 1)
    col = lax.broadcasted_iota(jnp.int32, (1, 2 * A_BLOCK), 1)
    no_prev = jnp.where(jnp.logical_and(first, col < A_BLOCK), NEG, 0.0).astype(F32)
    for qb in range(tq // A_BLOCK):
        rows = slice(qb * A_BLOCK, (qb + 1) * A_BLOCK)
        krows = slice(qb * A_BLOCK, (qb + 2) * A_BLOCK)
        for pair in range(A_HEADS // 2):
            cols = slice(pair * LANES, (pair + 1) * LANES)
            qp = q_ref[0, rows, cols]
            kp = kbuf[krows, cols]
            vp = vbuf[krows, cols]
            o_acc = jnp.zeros((A_BLOCK, LANES), F32)
            lse_acc = jnp.zeros((A_BLOCK, LANES), F32)
            for hh in range(2):
                in_head = (lane >= hh * A_HEAD_DIM) & (lane < (hh + 1) * A_HEAD_DIM)
                qm = jnp.where(in_head, qp, jnp.zeros_like(qp))
                s = lax.dot_general(qm, kp, (((1,), (1,)), ((), ())), preferred_element_type=F32)
                s = s + bias_ref[2 * pair + hh]
                if qb == 0:
                    s = s + no_prev
                m = jnp.max(s, axis=-1, keepdims=True)
                e = jnp.exp(s - m)
                den = jnp.sum(e, axis=-1, keepdims=True)
                vm = jnp.where(in_head, vp, jnp.zeros_like(vp))
                r = _dot(e.astype(BF16), vm)
                o_acc = o_acc + r * (1.0 / den)
                lse_acc = jnp.where(in_head, m + jnp.log(den), lse_acc)
            o_ref[0, rows, cols] = o_acc
            lse_ref[0, rows, cols] = lse_acc


def _attn_group(qkv, bias, g, d, *, tq):
    B, S, N3 = qkv.shape
    L = S // d
    W = A_GROUP_WIDTH
    ncb = N3 // W
    view = qkv.reshape(B, L, d * N3)
    tq = min(tq, L)
    bpt = tq // A_BLOCK
    cur = lambda off: pl.BlockSpec((1, tq, W), lambda b, r, i: (b, i, r * ncb + off + g))
    prev = lambda off: pl.BlockSpec(
        (1, A_BLOCK, W), lambda b, r, i: (b, jnp.maximum(i * bpt - 1, 0), r * ncb + off + g))
    out = pl.BlockSpec((1, tq, W), lambda b, r, i: (b, i, r))
    o, lse = pl.pallas_call(
        functools.partial(_attn_body, tq=tq),
        grid=(B, d, L // tq),
        in_specs=[cur(0), cur(A_GROUPS), prev(A_GROUPS), cur(2 * A_GROUPS), prev(2 * A_GROUPS),
                  _resident(bias.shape)],
        out_specs=[out, out],
        out_shape=[jax.ShapeDtypeStruct((B, L, d * W), F32)] * 2,
        scratch_shapes=[pltpu.VMEM((tq + A_BLOCK, W), BF16)] * 2,
        compiler_params=_params(3),
        name=f"attn_d{d}",
    )(view, view, view, view, view, bias)
    return o.reshape(B, S, W), lse.reshape(B, S, W)


def _combine_body(h_ref, o0, o1, o2, l0, l1, l2, w_ref, out_ref):
    la, lb, lc = l0[0], l1[0], l2[0]
    m = jnp.maximum(jnp.maximum(la, lb), lc)
    ea, eb, ec = jnp.exp(la - m), jnp.exp(lb - m), jnp.exp(lc - m)
    o = (ea * o0[0] + eb * o1[0] + ec * o2[0]) / (ea + eb + ec)
    out_ref[0] = h_ref[0] + _dot(o.astype(BF16), w_ref[...])


def _combine_wo(h, outs, lses, w_o, *, tm):
    B, S, D = h.shape
    W = A_GROUP_WIDTH
    tile = lambda w: pl.BlockSpec((1, tm, w), lambda b, i: (b, i, 0))
    return pl.pallas_call(
        _combine_body,
        grid=(B, S // tm),
        in_specs=[tile(D)] + [tile(W)] * 6 + [_resident(w_o.shape)],
        out_specs=tile(D),
        out_shape=jax.ShapeDtypeStruct((B, S, D), F32),
        compiler_params=_params(2),
        name="attn_combine",
    )(h, *outs, *lses, w_o)


def _t5_bucket(n):
    nf = np.maximum(n, 1).astype(np.float32)
    large = REL_MAX_EXACT + (np.log(nf / REL_MAX_EXACT) / np.log(REL_MAX_DIST / REL_MAX_EXACT)
                             * (REL_BUCKETS - REL_MAX_EXACT)).astype(np.int32)
    large = np.minimum(large, REL_BUCKETS - 1)
    return np.where(n < REL_MAX_EXACT, n, large).astype(np.int32)


def _band_bias(rel_table, g, d):
    i = np.arange(A_BLOCK)[:, None]
    j = np.arange(2 * A_BLOCK)[None, :]
    rel = A_BLOCK + i - j
    valid = (rel >= 0) & (rel <= A_BLOCK)
    bucket = _t5_bucket(np.where(valid, rel, 0) * d)
    table_g = rel_table[:, g * A_HEADS:(g + 1) * A_HEADS]
    bias = jnp.transpose(table_g[bucket], (2, 0, 1))
    return jnp.where(valid[None], bias, NEG).astype(F32)


def _mixer_dilated(h, g_mix, w_qkv, w_o, rel_table, *, tm, tq):
    qkv = _qkv_proj(h, g_mix, w_qkv, tm=tm)
    outs, lses = [], []
    for g, d in enumerate(A_DILATIONS):
        o, lse = _attn_group(qkv, _band_bias(rel_table, g, d), g, d, tq=tq)
        outs.append(o)
        lses.append(lse)
    return _combine_wo(h, outs, lses, w_o, tm=tm)


def _pool_body(h_ref, halo_ref, g_ref, win_ref, wgrp_ref, sc_ref, wout_ref, o_ref, *, tm):
    i = pl.program_id(1)
    h = h_ref[0]
    gain = g_ref[...]
    y_t = _dot(_rms(h, gain).astype(BF16), win_ref[...])
    y_h = _dot(_rms(halo_ref[0], gain).astype(BF16), win_ref[...])
    y_h = jnp.where(i == 0, 0.0, y_h)
    y = jnp.concatenate([y_h, y_t], axis=0)
    pos = (lax.broadcasted_iota(jnp.int32, (tm, 1), 0) + (i * tm + 1)).astype(F32)
    gw = y.shape[1] // len(B_WINDOWS)
    zs = []
    for g, win in enumerate(B_WINDOWS):
        yg = y[:, g * gw:(g + 1) * gw]
        s = yg
        sh = 1
        while sh < win:
            s = s + pltpu.roll(s, sh, axis=0)
            sh *= 2
        mean = s[B_HALO:] * (1.0 / jnp.minimum(pos, float(win)))
        pooled = mean - yg[B_HALO:]
        zs.append(_dot(pooled.astype(BF16), wgrp_ref[g]))
    z = jnp.concatenate(zs, axis=1) * sc_ref[...]
    o_ref[0] = h + _dot(z.astype(BF16), wout_ref[...])


def _mixer_pool(h, g_mix, w_in, w_grp, scale, w_out, *, tm):
    B, S, D = h.shape
    hpt = tm // B_HALO
    return pl.pallas_call(
        functools.partial(_pool_body, tm=tm),
        grid=(B, S // tm),
        in_specs=[pl.BlockSpec((1, tm, D), lambda b, i: (b, i, 0)),
                  pl.BlockSpec((1, B_HALO, D), lambda b, i: (b, jnp.maximum(i * hpt - 1, 0), 0)),
                  _resident((1, D)), _resident(w_in.shape), _resident(w_grp.shape),
                  _resident((1, D)), _resident(w_out.shape)],
        out_specs=pl.BlockSpec((1, tm, D), lambda b, i: (b, i, 0)),
        out_shape=jax.ShapeDtypeStruct((B, S, D), F32),
        compiler_params=_params(2),
        name="mixer_pool",
    )(h, h, g_mix, w_in, w_grp, scale, w_out)


def _sgu_body(h_ref, g_ref, win_ref, vg_ref, ws_ref, bs_ref, wout_ref, o_ref, *, tm):
    h = h_ref[0]
    D = h.shape[1]
    xn = _rms(h, g_ref[...]).astype(BF16)
    u = jax.nn.gelu(_dot(xn, win_ref[:, :D]))
    v = jax.nn.gelu(_dot(xn, win_ref[:, D:]))
    mu = jnp.mean(v, axis=-1, keepdims=True)
    var = jnp.mean(jnp.square(v - mu), axis=-1, keepdims=True)
    vn = ((v - mu) * lax.rsqrt(var + EPS) * vg_ref[...]).astype(BF16)
    t = lax.broadcasted_iota(jnp.int32, (C_CHUNK, C_CHUNK), 0)
    s = lax.broadcasted_iota(jnp.int32, (C_CHUNK, C_CHUNK), 1)
    gw = D // C_GROUPS
    cols = []
    for g in range(C_GROUPS):
        wm = jnp.where(s <= t, ws_ref[g], 0.0).astype(BF16)
        b = bs_ref[:, g:g + 1]
        rows = []
        for n in range(tm // C_CHUNK):
            vc = vn[n * C_CHUNK:(n + 1) * C_CHUNK, g * gw:(g + 1) * gw]
            rows.append(_dot(wm, vc) + b)
        cols.append(jnp.concatenate(rows, axis=0))
    sp = jnp.concatenate(cols, axis=1)
    o_ref[0] = h + _dot((u * sp).astype(BF16), wout_ref[...])


def _mixer_sgu(h, g_mix, w_in, v_gain, w_s, b_s_t, w_out, *, tm):
    B, S, D = h.shape
    return pl.pallas_call(
        functools.partial(_sgu_body, tm=tm),
        grid=(B, S // tm),
        in_specs=[pl.BlockSpec((1, tm, D), lambda b, i: (b, i, 0)),
                  _resident((1, D)), _resident(w_in.shape), _resident((1, D)),
                  _resident(w_s.shape), _resident(b_s_t.shape), _resident(w_out.shape)],
        out_specs=pl.BlockSpec((1, tm, D), lambda b, i: (b, i, 0)),
        out_shape=jax.ShapeDtypeStruct((B, S, D), F32),
        compiler_params=_params(2),
        name="mixer_sgu",
    )(h, g_mix, w_in, v_gain, w_s, b_s_t, w_out)


def _chunk_ffn_weights(w_gate, w_up, w_down, th):
    D, Hd = w_gate.shape
    C = Hd // th
    wg = w_gate.reshape(D, C, th).transpose(1, 0, 2)
    wu = w_up.reshape(D, C, th).transpose(1, 0, 2)
    wgu = jnp.concatenate([wg, wu], axis=2).astype(BF16)
    wd = w_down.reshape(C, th, D).astype(BF16)
    return wgu, wd


def kernel(x, p, rel_table, norm_mix, norm_ffn, norm_ple, final_norm, a_w_qkv, a_w_o, b_w_in, b_w_grp, b_scale, b_w_out, c_w_in, c_v_gain, c_w_s, c_b_s, c_w_out, ffn_w_gate, ffn_w_up, ffn_w_down, ple_w_gate, ple_w_proj):
    depth = norm_mix.shape[0]
    D = x.shape[-1]
    row = lambda v: v.reshape(1, D)
    tm = 512
    h = x
    for i in range(depth):
        kind, j = i % 3, i // 3
        if kind == 0:
            h = _mixer_dilated(h, row(norm_mix[i]), a_w_qkv[j].astype(BF16), a_w_o[j].astype(BF16),
                               rel_table, tm=tm, tq=512)
        elif kind == 1:
            h = _mixer_pool(h, row(norm_mix[i]), b_w_in[j].astype(BF16), b_w_grp[j].astype(BF16),
                            row(b_scale[j]), b_w_out[j].astype(BF16), tm=tm)
        else:
            h = _mixer_sgu(h, row(norm_mix[i]), c_w_in[j].astype(BF16), row(c_v_gain[j]),
                           c_w_s[j], jnp.transpose(c_b_s[j]), c_w_out[j].astype(BF16), tm=tm)
        wgu, wd = _chunk_ffn_weights(ffn_w_gate[i], ffn_w_up[i], ffn_w_down[i], th=256)
        h = _ffn_ple(h, p[i], row(norm_ffn[i]), wgu, wd, row(norm_ple[i]),
                     ple_w_gate[i].astype(BF16), ple_w_proj[i].astype(BF16), row(final_norm),
                     final_norm=(i == depth - 1), tm=tm)
    return h
```

```python
import functools

import jax
import jax.numpy as jnp
import numpy as np
from jax import lax
from jax.experimental import pallas as pl
from jax.experimental.pallas import tpu as pltpu

EPS = 1e-6
NEG = -1e30

A_DILATIONS = (1, 4, 16)
A_GROUPS = 3
A_HEADS = 8
A_HEAD_DIM = 64
A_GROUP_WIDTH = A_HEADS * A_HEAD_DIM
A_BLOCK = 128
REL_BUCKETS = 32
REL_MAX_EXACT = REL_BUCKETS // 2
REL_MAX_DIST = 2048
B_WINDOWS = (2, 4, 8, 16)
B_HALO = 16
C_CHUNK = 128
C_GROUPS = 4

LANES = 128
V7X_VMEM_LIMIT_BYTES = 56 * 1024 * 1024

BF16 = jnp.bfloat16
F32 = jnp.float32


def _dot(a, b):
    return jnp.dot(a, b, preferred_element_type=F32)


def _rms(x, g):
    ms = jnp.mean(x * x, axis=-1, keepdims=True)
    return x * lax.rsqrt(ms + EPS) * g


def _resident(shape):
    nd = len(shape)
    return pl.BlockSpec(shape, lambda *_: (0,) * nd, pipeline_mode=pl.Buffered(1))


def _params(n_grid_dims):
    return pltpu.CompilerParams(
        dimension_semantics=("parallel",) * n_grid_dims,
        vmem_limit_bytes=V7X_VMEM_LIMIT_BYTES,
    )


def _ffn_ple_body(h_ref, p_ref, gf_ref, wgu_ref, wd_ref, gp_ref, wpg_ref, wpp_ref, gl_ref,
                  o_ref, *, n_chunks, th, final_norm):
    h = h_ref[0]
    xn = _rms(h, gf_ref[...]).astype(BF16)
    acc = jnp.zeros_like(h)
    for c in range(n_chunks):
        gu = _dot(xn, wgu_ref[c])
        g = gu[:, :th]
        u = gu[:, th:]
        a = (g * jax.nn.sigmoid(g) * u).astype(BF16)
        acc = acc + _dot(a, wd_ref[c])
    h = h + acc
    gate = jax.nn.sigmoid(_dot(_rms(h, gp_ref[...]).astype(BF16), wpg_ref[...]))
    proj = _dot(p_ref[0].astype(BF16), wpp_ref[...])
    h = h + proj * gate
    if final_norm:
        h = _rms(h, gl_ref[...])
    o_ref[0] = h


def _ffn_ple(h, p, g_ffn, wgu, wd, g_ple, w_pg, w_pp, g_last, *, final_norm, tm):
    B, S, D = h.shape
    P = p.shape[-1]
    n_chunks, _, th2 = wgu.shape
    body = functools.partial(_ffn_ple_body, n_chunks=n_chunks, th=th2 // 2, final_norm=final_norm)
    tile = lambda w: pl.BlockSpec((1, tm, w), lambda b, i: (b, i, 0))
    return pl.pallas_call(
        body,
        grid=(B, S // tm),
        in_specs=[tile(D), tile(P), _resident((1, D)), _resident(wgu.shape), _resident(wd.shape),
                  _resident((1, D)), _resident(w_pg.shape), _resident(w_pp.shape), _resident((1, D))],
        out_specs=tile(D),
        out_shape=jax.ShapeDtypeStruct((B, S, D), F32),
        compiler_params=_params(2),
        name="ffn_ple",
    )(h, p, g_ffn, wgu, wd, g_ple, w_pg, w_pp, g_last)


def _qkv_body(h_ref, g_ref, w_ref, *refs, tm):
    outs, ybuf = refs[:A_GROUPS], refs[A_GROUPS]
    xn = _rms(h_ref[0], g_ref[...]).astype(BF16)
    W = A_GROUP_WIDTH
    for c in range(3 * A_GROUPS):
        part, g = divmod(c, A_GROUPS)
        d = A_DILATIONS[g]
        y = _dot(xn, w_ref[:, c * W:(c + 1) * W])
        if part == 0:
            y = y * (A_HEAD_DIM ** -0.5)
        if d == 1:
            outs[g][0, 0, :, part * W:(part + 1) * W] = y.astype(BF16)
            continue
        for s in range(W // LANES):
            ybuf[s] = y[:, s * LANES:(s + 1) * LANES]
        for r in range(d):
            for s in range(W // LANES):
                lo = part * W + s * LANES
                outs[g][0, r, :, lo:lo + LANES] = ybuf[s, pl.ds(r, tm // d, stride=d), :].astype(BF16)


def _qkv_proj(h, g, w, *, tm):
    B, S, D = h.shape
    W3 = 3 * A_GROUP_WIDTH
    return pl.pallas_call(
        functools.partial(_qkv_body, tm=tm),
        grid=(B, S // tm),
        in_specs=[pl.BlockSpec((1, tm, D), lambda b, i: (b, i, 0)), _resident((1, D)), _resident(w.shape)],
        out_specs=[pl.BlockSpec((1, d, tm // d, W3), lambda b, i: (b, 0, i, 0)) for d in A_DILATIONS],
        out_shape=[jax.ShapeDtypeStruct((B, d, S // d, W3), BF16) for d in A_DILATIONS],
        scratch_shapes=[pltpu.VMEM((A_GROUP_WIDTH // LANES, tm, LANES), F32)],
        compiler_params=_params(2),
        name="qkv_proj",
    )(h, g, w)


def _attn_body(q_ref, kc_ref, kp_ref, vc_ref, vp_ref, bias_ref, o_ref, lse_ref, kbuf, vbuf, *, tq):
    first = pl.program_id(2) == 0
    kbuf[:A_BLOCK] = kp_ref[0, 0]
    kbuf[A_BLOCK:] = kc_ref[0, 0]
    vbuf[:A_BLOCK] = vp_ref[0, 0]
    vbuf[A_BLOCK:] = vc_ref[0, 0]
    lane = lax.broadcasted_iota(jnp.int32, (1, LANES), 1)
    col = lax.broadcasted_iota(jnp.int32, (1, 2 * A_BLOCK), 1)
    no_prev = jnp.where(jnp.logical_and(first, col < A_BLOCK), NEG, 0.0).astype(F32)
    for qb in range(tq // A_BLOCK):
        rows = slice(qb * A_BLOCK, (qb + 1) * A_BLOCK)
        krows = slice(qb * A_BLOCK, (qb + 2) * A_BLOCK)
        for pair in range(A_HEADS // 2):
            cols = slice(pair * LANES, (pair + 1) * LANES)
            qp = q_ref[0, 0, rows, cols]
            kp = kbuf[krows, cols]
            vp = vbuf[krows, cols]
            o_acc = jnp.zeros((A_BLOCK, LANES), F32)
            lse_acc = jnp.zeros((A_BLOCK, LANES), F32)
            for hh in range(2):
                in_head = (lane >= hh * A_HEAD_DIM) & (lane < (hh + 1) * A_HEAD_DIM)
                qm = jnp.where(in_head, qp, jnp.zeros_like(qp))
                s = lax.dot_general(qm, kp, (((1,), (1,)), ((), ())), preferred_element_type=F32)
                s = s + bias_ref[0, 2 * pair + hh]
                if qb == 0:
                    s = s + no_prev
                m = jnp.max(s, axis=-1, keepdims=True)
                e = jnp.exp(s - m)
                den = jnp.sum(e, axis=-1, keepdims=True)
                vm = jnp.where(in_head, vp, jnp.zeros_like(vp))
                r = _dot(e.astype(BF16), vm)
                o_acc = o_acc + r * (1.0 / den)
                lse_acc = jnp.where(in_head, m + jnp.log(den), lse_acc)
            o_ref[0, 0, rows, cols] = o_acc
            lse_ref[0, 0, rows, cols] = lse_acc


def _attn_group(qkv_g, bias, g, *, tq):
    B, d, L, _ = qkv_g.shape
    W = A_GROUP_WIDTH
    tq = min(tq, L)
    bpt = tq // A_BLOCK
    cur = lambda part: pl.BlockSpec((1, 1, tq, W), lambda b, r, i: (b, r, i, part))
    prev = lambda part: pl.BlockSpec(
        (1, 1, A_BLOCK, W), lambda b, r, i: (b, r, jnp.maximum(i * bpt - 1, 0), part))
    out = pl.BlockSpec((1, 1, tq, W), lambda b, r, i: (b, r, i, 0))
    bias_spec = pl.BlockSpec((1,) + bias.shape[1:], lambda b, r, i: (g, 0, 0, 0),
                             pipeline_mode=pl.Buffered(1))
    return pl.pallas_call(
        functools.partial(_attn_body, tq=tq),
        grid=(B, d, L // tq),
        in_specs=[cur(0), cur(1), prev(1), cur(2), prev(2), bias_spec],
        out_specs=[out, out],
        out_shape=[jax.ShapeDtypeStruct((B, d, L, W), F32)] * 2,
        scratch_shapes=[pltpu.VMEM((tq + A_BLOCK, W), BF16)] * 2,
        compiler_params=_params(3),
        name=f"attn_d{d}",
    )(qkv_g, qkv_g, qkv_g, qkv_g, qkv_g, bias)


def _combine_body(h_ref, o0, o1, o2, l0, l1, l2, w_ref, out_ref, *bufs, tm):
    def token_order(ref, buf, s):
        d = ref.shape[1]
        if d == 1:
            return ref[0, 0, :, s * LANES:(s + 1) * LANES]
        for r in range(d):
            buf[pl.ds(r, tm // d, stride=d), :] = ref[0, r, :, s * LANES:(s + 1) * LANES]
        return buf[...]

    slabs = []
    for s in range(A_GROUP_WIDTH // LANES):
        la, lb, lc = token_order(l0, None, s), token_order(l1, bufs[0], s), token_order(l2, bufs[1], s)
        oa, ob, oc = token_order(o0, None, s), token_order(o1, bufs[2], s), token_order(o2, bufs[3], s)
        m = jnp.maximum(jnp.maximum(la, lb), lc)
        ea, eb, ec = jnp.exp(la - m), jnp.exp(lb - m), jnp.exp(lc - m)
        slabs.append(((ea * oa + eb * ob + ec * oc) / (ea + eb + ec)).astype(BF16))
    out_ref[0] = h_ref[0] + _dot(jnp.concatenate(slabs, axis=1), w_ref[...])


def _combine_wo(h, outs, lses, w_o, *, tm):
    B, S, D = h.shape
    W = A_GROUP_WIDTH
    tile = pl.BlockSpec((1, tm, D), lambda b, i: (b, i, 0))
    group_specs = [pl.BlockSpec((1, d, tm // d, W), lambda b, i: (b, 0, i, 0)) for d in A_DILATIONS]
    return pl.pallas_call(
        functools.partial(_combine_body, tm=tm),
        grid=(B, S // tm),
        in_specs=[tile] + group_specs + group_specs + [_resident(w_o.shape)],
        out_specs=tile,
        out_shape=jax.ShapeDtypeStruct((B, S, D), F32),
        scratch_shapes=[pltpu.VMEM((tm, LANES), F32)] * 4,
        compiler_params=_params(2),
        name="attn_combine",
    )(h, *outs, *lses, w_o)


def _t5_bucket(n):
    nf = np.maximum(n, 1).astype(np.float32)
    large = REL_MAX_EXACT + (np.log(nf / REL_MAX_EXACT) / np.log(REL_MAX_DIST / REL_MAX_EXACT)
                             * (REL_BUCKETS - REL_MAX_EXACT)).astype(np.int32)
    large = np.minimum(large, REL_BUCKETS - 1)
    return np.where(n < REL_MAX_EXACT, n, large).astype(np.int32)


def _band_buckets():
    i = np.arange(A_BLOCK)[:, None]
    j = np.arange(2 * A_BLOCK)[None, :]
    rel = A_BLOCK + i - j
    valid = (rel >= 0) & (rel <= A_BLOCK)
    return np.stack([np.where(valid, _t5_bucket(np.where(valid, rel, 0) * d), -1)
                     for d in A_DILATIONS]).astype(np.int32)


def _bias_body(table_ref, bucket_ref, o_ref):
    g = pl.program_id(0)
    bk = bucket_ref[0]
    for hd in range(A_HEADS):
        acc = jnp.full(bk.shape, NEG, F32)
        for b in range(REL_BUCKETS):
            acc = jnp.where(bk == b, table_ref[b, g * A_HEADS + hd], acc)
        o_ref[0, hd] = acc


def _band_bias(rel_table):
    buckets = jnp.asarray(_band_buckets())
    G, Q, Kk = buckets.shape
    return pl.pallas_call(
        _bias_body,
        grid=(G,),
        in_specs=[pl.BlockSpec(memory_space=pltpu.SMEM),
                  pl.BlockSpec((1, Q, Kk), lambda g: (g, 0, 0))],
        out_specs=pl.BlockSpec((1, A_HEADS, Q, Kk), lambda g: (g, 0, 0, 0)),
        out_shape=jax.ShapeDtypeStruct((G, A_HEADS, Q, Kk), F32),
        compiler_params=_params(1),
        name="band_bias",
    )(rel_table, buckets)


def _mixer_dilated(h, g_mix, w_qkv, w_o, bias, *, tm, tq):
    qkv = _qkv_proj(h, g_mix, w_qkv, tm=tm)
    outs, lses = [], []
    for g in range(A_GROUPS):
        o, lse = _attn_group(qkv[g], bias, g, tq=tq)
        outs.append(o)
        lses.append(lse)
    return _combine_wo(h, outs, lses, w_o, tm=tm)


def _pool_body(h_ref, halo_ref, g_ref, win_ref, wgrp_ref, sc_ref, wout_ref, o_ref, *, tm):
    i = pl.program_id(1)
    h = h_ref[0]
    gain = g_ref[...]
    y_t = _dot(_rms(h, gain).astype(BF16), win_ref[...])
    y_h = _dot(_rms(halo_ref[0], gain).astype(BF16), win_ref[...])
    y_h = jnp.where(i == 0, 0.0, y_h)
    y = jnp.concatenate([y_h, y_t], axis=0)
    pos = (lax.broadcasted_iota(jnp.int32, (tm, 1), 0) + (i * tm + 1)).astype(F32)
    gw = y.shape[1] // len(B_WINDOWS)
    zs = []
    for g, win in enumerate(B_WINDOWS):
        yg = y[:, g * gw:(g + 1) * gw]
        s = yg
        sh = 1
        while sh < win:
            s = s + pltpu.roll(s, sh, axis=0)
            sh *= 2
        mean = s[B_HALO:] * (1.0 / jnp.minimum(pos, float(win)))
        pooled = mean - yg[B_HALO:]
        zs.append(_dot(pooled.astype(BF16), wgrp_ref[g]))
    z = jnp.concatenate(zs, axis=1) * sc_ref[...]
    o_ref[0] = h + _dot(z.astype(BF16), wout_ref[...])


def _mixer_pool(h, g_mix, w_in, w_grp, scale, w_out, *, tm):
    B, S, D = h.shape
    hpt = tm // B_HALO
    return pl.pallas_call(
        functools.partial(_pool_body, tm=tm),
        grid=(B, S // tm),
        in_specs=[pl.BlockSpec((1, tm, D), lambda b, i: (b, i, 0)),
                  pl.BlockSpec((1, B_HALO, D), lambda b, i: (b, jnp.maximum(i * hpt - 1, 0), 0)),
                  _resident((1, D)), _resident(w_in.shape), _resident(w_grp.shape),
                  _resident((1, D)), _resident(w_out.shape)],
        out_specs=pl.BlockSpec((1, tm, D), lambda b, i: (b, i, 0)),
        out_shape=jax.ShapeDtypeStruct((B, S, D), F32),
        compiler_params=_params(2),
        name="mixer_pool",
    )(h, h, g_mix, w_in, w_grp, scale, w_out)


def _sgu_body(h_ref, g_ref, win_ref, vg_ref, ws_ref, bs_ref, wout_ref, o_ref, *, tm):
    h = h_ref[0]
    D = h.shape[1]
    xn = _rms(h, g_ref[...]).astype(BF16)
    u = jax.nn.gelu(_dot(xn, win_ref[:, :D]))
    v = jax.nn.gelu(_dot(xn, win_ref[:, D:]))
    mu = jnp.mean(v, axis=-1, keepdims=True)
    var = jnp.mean(jnp.square(v - mu), axis=-1, keepdims=True)
    vn = ((v - mu) * lax.rsqrt(var + EPS) * vg_ref[...]).astype(BF16)
    t = lax.broadcasted_iota(jnp.int32, (C_CHUNK, C_CHUNK), 0)
    s = lax.broadcasted_iota(jnp.int32, (C_CHUNK, C_CHUNK), 1)
    gw = D // C_GROUPS
    cols = []
    for g in range(C_GROUPS):
        wm = jnp.where(s <= t, ws_ref[g], 0.0).astype(BF16)
        b = bs_ref[:, g:g + 1]
        rows = []
        for n in range(tm // C_CHUNK):
            vc = vn[n * C_CHUNK:(n + 1) * C_CHUNK, g * gw:(g + 1) * gw]
            rows.append(_dot(wm, vc) + b)
        cols.append(jnp.concatenate(rows, axis=0))
    sp = jnp.concatenate(cols, axis=1)
    o_ref[0] = h + _dot((u * sp).astype(BF16), wout_ref[...])


def _mixer_sgu(h, g_mix, w_in, v_gain, w_s, b_s_t, w_out, *, tm):
    B, S, D = h.shape
    return pl.pallas_call(
        functools.partial(_sgu_body, tm=tm),
        grid=(B, S // tm),
        in_specs=[pl.BlockSpec((1, tm, D), lambda b, i: (b, i, 0)),
                  _resident((1, D)), _resident(w_in.shape), _resident((1, D)),
                  _resident(w_s.shape), _resident(b_s_t.shape), _resident(w_out.shape)],
        out_specs=pl.BlockSpec((1, tm, D), lambda b, i: (b, i, 0)),
        out_shape=jax.ShapeDtypeStruct((B, S, D), F32),
        compiler_params=_params(2),
        name="mixer_sgu",
    )(h, g_mix, w_in, v_gain, w_s, b_s_t, w_out)


def _chunk_ffn_weights(w_gate, w_up, w_down, th):
    D, Hd = w_gate.shape
    C = Hd // th
    wg = w_gate.reshape(D, C, th).transpose(1, 0, 2)
    wu = w_up.reshape(D, C, th).transpose(1, 0, 2)
    wgu = jnp.concatenate([wg, wu], axis=2).astype(BF16)
    wd = w_down.reshape(C, th, D).astype(BF16)
    return wgu, wd


def kernel(x, p, rel_table, norm_mix, norm_ffn, norm_ple, final_norm, a_w_qkv, a_w_o, b_w_in, b_w_grp, b_scale, b_w_out, c_w_in, c_v_gain, c_w_s, c_b_s, c_w_out, ffn_w_gate, ffn_w_up, ffn_w_down, ple_w_gate, ple_w_proj):
    depth = norm_mix.shape[0]
    D = x.shape[-1]
    row = lambda v: v.reshape(1, D)
    tm = 512
    bias = _band_bias(rel_table)
    h = x
    for i in range(depth):
        kind, j = i % 3, i // 3
        if kind == 0:
            h = _mixer_dilated(h, row(norm_mix[i]), a_w_qkv[j].astype(BF16), a_w_o[j].astype(BF16),
                               bias, tm=tm, tq=512)
        elif kind == 1:
            h = _mixer_pool(h, row(norm_mix[i]), b_w_in[j].astype(BF16), b_w_grp[j].astype(BF16),
                            row(b_scale[j]), b_w_out[j].astype(BF16), tm=tm)
        else:
            h = _mixer_sgu(h, row(norm_mix[i]), c_w_in[j].astype(BF16), row(c_v_gain[j]),
                           c_w_s[j], jnp.transpose(c_b_s[j]), c_w_out[j].astype(BF16), tm=tm)
        wgu, wd = _chunk_ffn_weights(ffn_w_gate[i], ffn_w_up[i], ffn_w_down[i], th=256)
        h = _ffn_ple(h, p[i], row(norm_ffn[i]), wgu, wd, row(norm_ple[i]),
                     ple_w_gate[i].astype(BF16), ple_w_proj[i].astype(BF16), row(final_norm),
                     final_norm=(i == depth - 1), tm=tm)
    return h
```

```python
import functools

import jax
import jax.numpy as jnp
import numpy as np
from jax import lax
from jax.experimental import pallas as pl
from jax.experimental.pallas import tpu as pltpu

EPS = 1e-6
NEG = -1e30

A_DILATIONS = (1, 4, 16)
A_GROUPS = 3
A_HEADS = 8
A_HEAD_DIM = 64
A_GROUP_WIDTH = A_HEADS * A_HEAD_DIM
A_BLOCK = 128
A_SPAN = A_BLOCK * max(A_DILATIONS)
REL_BUCKETS = 32
REL_MAX_EXACT = REL_BUCKETS // 2
REL_MAX_DIST = 2048
B_WINDOWS = (2, 4, 8, 16)
B_HALO = 16
C_CHUNK = 128
C_GROUPS = 4

LANES = 128
V7X_VMEM_LIMIT_BYTES = 56 * 1024 * 1024

BF16 = jnp.bfloat16
F32 = jnp.float32


def _dot(a, b):
    return jnp.dot(a, b, preferred_element_type=F32)


def _dot_t(a, b):
    return lax.dot_general(a, b, (((1,), (1,)), ((), ())), preferred_element_type=F32)


def _rms(x, g):
    ms = jnp.mean(x * x, axis=-1, keepdims=True)
    return x * lax.rsqrt(ms + EPS) * g


def _resident(shape):
    nd = len(shape)
    return pl.BlockSpec(shape, lambda *_: (0,) * nd, pipeline_mode=pl.Buffered(1))


def _params(n_grid_dims):
    return pltpu.CompilerParams(
        dimension_semantics=("parallel",) * n_grid_dims,
        vmem_limit_bytes=V7X_VMEM_LIMIT_BYTES,
    )


def _ffn_ple_body(h_ref, p_ref, gf_ref, wgu_ref, wd_ref, gp_ref, wpg_ref, wpp_ref, gl_ref,
                  *refs, n_chunks, th, final_norm, with_attn):
    o_ref = refs[-1]
    h = h_ref[0]
    if with_attn:
        a_ref, wo_ref = refs[:2]
        h = h + _dot(a_ref[0], wo_ref[...])
    xn = _rms(h, gf_ref[...]).astype(BF16)
    acc = jnp.zeros_like(h)
    for c in range(n_chunks):
        gu = _dot(xn, wgu_ref[c])
        g = gu[:, :th]
        u = gu[:, th:]
        a = (g * jax.nn.sigmoid(g) * u).astype(BF16)
        acc = acc + _dot(a, wd_ref[c])
    h = h + acc
    gate = jax.nn.sigmoid(_dot(_rms(h, gp_ref[...]).astype(BF16), wpg_ref[...]))
    proj = _dot(p_ref[0, 0].astype(BF16), wpp_ref[...])
    h = h + proj * gate
    if final_norm:
        h = _rms(h, gl_ref[...])
    o_ref[0] = h


def _ffn_ple(h, p, layer, g_ffn, wgu, wd, g_ple, w_pg, w_pp, g_last, attn=None, *, final_norm, tm):
    B, S, D = h.shape
    P = p.shape[-1]
    n_chunks, _, th2 = wgu.shape
    body = functools.partial(_ffn_ple_body, n_chunks=n_chunks, th=th2 // 2, final_norm=final_norm,
                             with_attn=attn is not None)
    tile = lambda w: pl.BlockSpec((1, tm, w), lambda b, i: (b, i, 0))
    in_specs = [tile(D), pl.BlockSpec((1, 1, tm, P), lambda b, i: (layer, b, i, 0)),
                _resident((1, D)), _resident(wgu.shape), _resident(wd.shape),
                _resident((1, D)), _resident(w_pg.shape), _resident(w_pp.shape), _resident((1, D))]
    args = [h, p, g_ffn, wgu, wd, g_ple, w_pg, w_pp, g_last]
    if attn is not None:
        in_specs += [tile(attn[0].shape[-1]), _resident(attn[1].shape)]
        args += list(attn)
    return pl.pallas_call(
        body,
        grid=(B, S // tm),
        in_specs=in_specs,
        out_specs=tile(D),
        out_shape=jax.ShapeDtypeStruct((B, S, D), F32),
        compiler_params=_params(2),
        name="ffn_ple",
    )(*args)


def _qkv_body(h_ref, g_ref, w_ref, *refs, tm):
    outs, ybuf = refs[:A_GROUPS], refs[A_GROUPS]
    xn = _rms(h_ref[0], g_ref[...]).astype(BF16)
    W = A_GROUP_WIDTH
    for c in range(3 * A_GROUPS):
        part, g = divmod(c, A_GROUPS)
        d = A_DILATIONS[g]
        y = _dot(xn, w_ref[:, c * W:(c + 1) * W])
        if part == 0:
            y = y * (A_HEAD_DIM ** -0.5)
        if d == 1:
            outs[g][0, 0, :, part * W:(part + 1) * W] = y.astype(BF16)
            continue
        for s in range(W // LANES):
            ybuf[s] = y[:, s * LANES:(s + 1) * LANES]
        for r in range(d):
            for s in range(W // LANES):
                lo = part * W + s * LANES
                outs[g][0, r, :, lo:lo + LANES] = ybuf[s, pl.ds(r, tm // d, stride=d), :].astype(BF16)


def _qkv_proj(h, g, w, *, tm):
    B, S, D = h.shape
    W3 = 3 * A_GROUP_WIDTH
    return pl.pallas_call(
        functools.partial(_qkv_body, tm=tm),
        grid=(B, S // tm),
        in_specs=[pl.BlockSpec((1, tm, D), lambda b, i: (b, i, 0)), _resident((1, D)), _resident(w.shape)],
        out_specs=[pl.BlockSpec((1, d, tm // d, W3), lambda b, i: (b, 0, i, 0)) for d in A_DILATIONS],
        out_shape=[jax.ShapeDtypeStruct((B, d, S // d, W3), BF16) for d in A_DILATIONS],
        scratch_shapes=[pltpu.VMEM((A_GROUP_WIDTH // LANES, tm, LANES), F32)],
        compiler_params=_params(2),
        name="qkv_proj",
    )(h, g, w)


def _attn_body(*refs):
    n_in = 5 * A_GROUPS
    bias_ref, out_ref = refs[n_in], refs[n_in + 1]
    o_bufs, l_bufs = refs[n_in + 2:n_in + 2 + A_GROUPS], refs[n_in + 2 + A_GROUPS:]
    first = pl.program_id(1) == 0
    pair = pl.program_id(2)
    head0 = lax.broadcasted_iota(jnp.int32, (1, LANES), 1) < A_HEAD_DIM
    col = lax.broadcasted_iota(jnp.int32, (1, 2 * A_BLOCK), 1)
    no_prev = jnp.where(jnp.logical_and(first, col < A_BLOCK), NEG, 0.0).astype(F32)
    for g, d in enumerate(A_DILATIONS):
        q_ref, kc_ref, kp_ref, vc_ref, vp_ref = refs[5 * g:5 * g + 5]
        bias2 = bias_ref[g, pl.ds(2 * pair, 2)].reshape(2 * A_BLOCK, 2 * A_BLOCK)
        bias2_first = bias2 + no_prev
        for r in range(d):
            for qb in range(A_SPAN // d // A_BLOCK):
                q = q_ref[0, r, qb * A_BLOCK:(qb + 1) * A_BLOCK, :]
                if qb == 0:
                    keys = jnp.concatenate([kp_ref[0, r], kc_ref[0, r, :A_BLOCK, :]], axis=0)
                    vals = jnp.concatenate([vp_ref[0, r], vc_ref[0, r, :A_BLOCK, :]], axis=0)
                else:
                    keys = kc_ref[0, r, (qb - 1) * A_BLOCK:(qb + 1) * A_BLOCK, :]
                    vals = vc_ref[0, r, (qb - 1) * A_BLOCK:(qb + 1) * A_BLOCK, :]
                zero = jnp.zeros_like(q)
                q2 = jnp.concatenate([jnp.where(head0, q, zero), jnp.where(head0, zero, q)], axis=0)
                s = _dot_t(q2, keys) + (bias2_first if qb == 0 else bias2)
                m = jnp.max(s, axis=-1, keepdims=True)
                e = jnp.exp(s - m)
                den = jnp.sum(e, axis=-1, keepdims=True)
                pv = _dot(e.astype(BF16), vals) * (1.0 / den)
                lse = m + jnp.log(den)
                o_pair = jnp.where(head0, pv[:A_BLOCK], pv[A_BLOCK:])
                l_pair = jnp.where(head0, lse[:A_BLOCK], lse[A_BLOCK:])
                rows = pl.ds(qb * A_BLOCK * d + r, A_BLOCK, stride=d) if d > 1 else pl.ds(qb * A_BLOCK, A_BLOCK)
                o_bufs[g][rows, :] = o_pair
                l_bufs[g][rows, :] = l_pair
    chunk = 2 * A_BLOCK
    for c in range(A_SPAN // chunk):
        rows = pl.ds(c * chunk, chunk)
        ls = [l_bufs[g][rows, :] for g in range(A_GROUPS)]
        m = functools.reduce(jnp.maximum, ls)
        es = [jnp.exp(l - m) for l in ls]
        num = sum(e * o_bufs[g][rows, :] for g, e in enumerate(es))
        out_ref[0, rows, :] = (num / sum(es)).astype(BF16)


def _attention(qkv, bias):
    B, _, S, _ = qkv[0].shape
    n_pairs = A_GROUP_WIDTH // LANES
    in_specs, args = [], []
    for g, d in enumerate(A_DILATIONS):
        steps = A_SPAN // d
        bps = steps // A_BLOCK
        cur = lambda part, steps=steps, d=d: pl.BlockSpec(
            (1, d, steps, LANES), lambda b, i, pr: (b, 0, i, part * n_pairs + pr))
        prev = lambda part, bps=bps, d=d: pl.BlockSpec(
            (1, d, A_BLOCK, LANES), lambda b, i, pr: (b, 0, jnp.maximum(i * bps - 1, 0), part * n_pairs + pr))
        in_specs += [cur(0), cur(1), prev(1), cur(2), prev(2)]
        args += [qkv[g]] * 5
    return pl.pallas_call(
        _attn_body,
        grid=(B, S // A_SPAN, n_pairs),
        in_specs=in_specs + [_resident(bias.shape)],
        out_specs=pl.BlockSpec((1, A_SPAN, LANES), lambda b, i, pr: (b, i, pr)),
        out_shape=jax.ShapeDtypeStruct((B, S, A_GROUP_WIDTH), BF16),
        scratch_shapes=[pltpu.VMEM((A_SPAN, LANES), F32)] * (2 * A_GROUPS),
        compiler_params=_params(3),
        name="attention",
    )(*args, bias)


def _t5_bucket(n):
    nf = np.maximum(n, 1).astype(np.float32)
    large = REL_MAX_EXACT + (np.log(nf / REL_MAX_EXACT) / np.log(REL_MAX_DIST / REL_MAX_EXACT)
                             * (REL_BUCKETS - REL_MAX_EXACT)).astype(np.int32)
    large = np.minimum(large, REL_BUCKETS - 1)
    return np.where(n < REL_MAX_EXACT, n, large).astype(np.int32)


def _band_buckets():
    i = np.arange(A_BLOCK)[:, None]
    j = np.arange(2 * A_BLOCK)[None, :]
    rel = A_BLOCK + i - j
    valid = (rel >= 0) & (rel <= A_BLOCK)
    return np.stack([np.where(valid, _t5_bucket(np.where(valid, rel, 0) * d), -1)
                     for d in A_DILATIONS]).astype(np.int32)


def _bias_body(table_ref, bucket_ref, o_ref):
    g = pl.program_id(0)
    bk = bucket_ref[0]
    for hd in range(A_HEADS):
        acc = jnp.full(bk.shape, NEG, F32)
        for b in range(REL_BUCKETS):
            acc = jnp.where(bk == b, table_ref[b, g * A_HEADS + hd], acc)
        o_ref[0, hd] = acc


def _band_bias(rel_table):
    buckets = jnp.asarray(_band_buckets())
    G, Q, Kk = buckets.shape
    return pl.pallas_call(
        _bias_body,
        grid=(G,),
        in_specs=[pl.BlockSpec(memory_space=pltpu.SMEM),
                  pl.BlockSpec((1, Q, Kk), lambda g: (g, 0, 0))],
        out_specs=pl.BlockSpec((1, A_HEADS, Q, Kk), lambda g: (g, 0, 0, 0)),
        out_shape=jax.ShapeDtypeStruct((G, A_HEADS, Q, Kk), F32),
        compiler_params=_params(1),
        name="band_bias",
    )(rel_table, buckets)


def _mixer_dilated(h, g_mix, w_qkv, bias, *, tm):
    return _attention(_qkv_proj(h, g_mix, w_qkv, tm=tm), bias)


def _pool_body(h_ref, halo_ref, g_ref, win_ref, wgrp_ref, sc_ref, wout_ref, o_ref, *, tm):
    i = pl.program_id(1)
    h = h_ref[0]
    gain = g_ref[...]
    y_t = _dot(_rms(h, gain).astype(BF16), win_ref[...])
    y_h = _dot(_rms(halo_ref[0], gain).astype(BF16), win_ref[...])
    y_h = jnp.where(i == 0, 0.0, y_h)
    y = jnp.concatenate([y_h, y_t], axis=0)
    pos = (lax.broadcasted_iota(jnp.int32, (tm, 1), 0) + (i * tm + 1)).astype(F32)
    gw = y.shape[1] // len(B_WINDOWS)
    zs = []
    for g, win in enumerate(B_WINDOWS):
        yg = y[:, g * gw:(g + 1) * gw]
        s = yg
        sh = 1
        while sh < win:
            s = s + pltpu.roll(s, sh, axis=0)
            sh *= 2
        mean = s[B_HALO:] * (1.0 / jnp.minimum(pos, float(win)))
        pooled = mean - yg[B_HALO:]
        zs.append(_dot(pooled.astype(BF16), wgrp_ref[g]))
    z = jnp.concatenate(zs, axis=1) * sc_ref[...]
    o_ref[0] = h + _dot(z.astype(BF16), wout_ref[...])


def _mixer_pool(h, g_mix, w_in, w_grp, scale, w_out, *, tm):
    B, S, D = h.shape
    hpt = tm // B_HALO
    return pl.pallas_call(
        functools.partial(_pool_body, tm=tm),
        grid=(B, S // tm),
        in_specs=[pl.BlockSpec((1, tm, D), lambda b, i: (b, i, 0)),
                  pl.BlockSpec((1, B_HALO, D), lambda b, i: (b, jnp.maximum(i * hpt - 1, 0), 0)),
                  _resident((1, D)), _resident(w_in.shape), _resident(w_grp.shape),
                  _resident((1, D)), _resident(w_out.shape)],
        out_specs=pl.BlockSpec((1, tm, D), lambda b, i: (b, i, 0)),
        out_shape=jax.ShapeDtypeStruct((B, S, D), F32),
        compiler_params=_params(2),
        name="mixer_pool",
    )(h, h, g_mix, w_in, w_grp, scale, w_out)


def _sgu_body(h_ref, g_ref, win_ref, vg_ref, ws_ref, bs_ref, wout_ref, o_ref, *, tm):
    h = h_ref[0]
    D = h.shape[1]
    xn = _rms(h, g_ref[...]).astype(BF16)
    u = jax.nn.gelu(_dot(xn, win_ref[:, :D]))
    v = jax.nn.gelu(_dot(xn, win_ref[:, D:]))
    mu = jnp.mean(v, axis=-1, keepdims=True)
    var = jnp.mean(jnp.square(v - mu), axis=-1, keepdims=True)
    vn = ((v - mu) * lax.rsqrt(var + EPS) * vg_ref[...]).astype(BF16)
    t = lax.broadcasted_iota(jnp.int32, (C_CHUNK, C_CHUNK), 0)
    s = lax.broadcasted_iota(jnp.int32, (C_CHUNK, C_CHUNK), 1)
    gw = D // C_GROUPS
    cols = []
    for g in range(C_GROUPS):
        wm = jnp.where(s <= t, ws_ref[g], 0.0).astype(BF16)
        b = bs_ref[:, g:g + 1]
        rows = []
        for n in range(tm // C_CHUNK):
            vc = vn[n * C_CHUNK:(n + 1) * C_CHUNK, g * gw:(g + 1) * gw]
            rows.append(_dot(wm, vc) + b)
        cols.append(jnp.concatenate(rows, axis=0))
    sp = jnp.concatenate(cols, axis=1)
    o_ref[0] = h + _dot((u * sp).astype(BF16), wout_ref[...])


def _mixer_sgu(h, g_mix, w_in, v_gain, w_s, b_s_t, w_out, *, tm):
    B, S, D = h.shape
    return pl.pallas_call(
        functools.partial(_sgu_body, tm=tm),
        grid=(B, S // tm),
        in_specs=[pl.BlockSpec((1, tm, D), lambda b, i: (b, i, 0)),
                  _resident((1, D)), _resident(w_in.shape), _resident((1, D)),
                  _resident(w_s.shape), _resident(b_s_t.shape), _resident(w_out.shape)],
        out_specs=pl.BlockSpec((1, tm, D), lambda b, i: (b, i, 0)),
        out_shape=jax.ShapeDtypeStruct((B, S, D), F32),
        compiler_params=_params(2),
        name="mixer_sgu",
    )(h, g_mix, w_in, v_gain, w_s, b_s_t, w_out)


def _chunk_ffn_weights(w_gate, w_up, w_down, th):
    D, Hd = w_gate.shape
    C = Hd // th
    wg = w_gate.reshape(D, C, th).transpose(1, 0, 2)
    wu = w_up.reshape(D, C, th).transpose(1, 0, 2)
    wgu = jnp.concatenate([wg, wu], axis=2).astype(BF16)
    wd = w_down.reshape(C, th, D).astype(BF16)
    return wgu, wd


def kernel(x, p, rel_table, norm_mix, norm_ffn, norm_ple, final_norm, a_w_qkv, a_w_o, b_w_in, b_w_grp, b_scale, b_w_out, c_w_in, c_v_gain, c_w_s, c_b_s, c_w_out, ffn_w_gate, ffn_w_up, ffn_w_down, ple_w_gate, ple_w_proj):
    depth = norm_mix.shape[0]
    D = x.shape[-1]
    row = lambda v: v.reshape(1, D)
    tm = 512
    bias = _band_bias(rel_table)
    h = x
    for i in range(depth):
        kind, j = i % 3, i // 3
        attn = None
        if kind == 0:
            attn = (_mixer_dilated(h, row(norm_mix[i]), a_w_qkv[j].astype(BF16), bias, tm=tm),
                    a_w_o[j].astype(BF16))
        elif kind == 1:
            h = _mixer_pool(h, row(norm_mix[i]), b_w_in[j].astype(BF16), b_w_grp[j].astype(BF16),
                            row(b_scale[j]), b_w_out[j].astype(BF16), tm=tm)
        else:
            h = _mixer_sgu(h, row(norm_mix[i]), c_w_in[j].astype(BF16), row(c_v_gain[j]),
                           c_w_s[j], jnp.transpose(c_b_s[j]), c_w_out[j].astype(BF16), tm=tm)
        wgu, wd = _chunk_ffn_weights(ffn_w_gate[i], ffn_w_up[i], ffn_w_down[i], th=256)
        h = _ffn_ple(h, p, i, row(norm_ffn[i]), wgu, wd, row(norm_ple[i]),
                     ple_w_gate[i].astype(BF16), ple_w_proj[i].astype(BF16), row(final_norm), attn,
                     final_norm=(i == depth - 1), tm=tm)
    return h
```

```python
import functools

import jax
import jax.numpy as jnp
import numpy as np
from jax import lax
from jax.experimental import pallas as pl
from jax.experimental.pallas import tpu as pltpu

EPS = 1e-6
NEG = -1e30

A_DILATIONS = (1, 4, 16)
A_GROUPS = 3
A_HEADS = 8
A_HEAD_DIM = 64
A_GROUP_WIDTH = A_HEADS * A_HEAD_DIM
A_BLOCK = 128
A_SPAN = A_BLOCK * max(A_DILATIONS)
REL_BUCKETS = 32
REL_MAX_EXACT = REL_BUCKETS // 2
REL_MAX_DIST = 2048
B_WINDOWS = (2, 4, 8, 16)
B_HALO = 16
C_CHUNK = 128
C_GROUPS = 4

LANES = 128
V7X_VMEM_LIMIT_BYTES = 56 * 1024 * 1024

BF16 = jnp.bfloat16
F32 = jnp.float32


def _dot(a, b):
    return jnp.dot(a, b, preferred_element_type=F32)


def _dot_t(a, b):
    return lax.dot_general(a, b, (((1,), (1,)), ((), ())), preferred_element_type=F32)


def _rms(x, g):
    ms = jnp.mean(x * x, axis=-1, keepdims=True)
    return x * lax.rsqrt(ms + EPS) * g


def _resident(shape):
    nd = len(shape)
    return pl.BlockSpec(shape, lambda *_: (0,) * nd, pipeline_mode=pl.Buffered(1))


def _layer(arr, layer):
    nd = arr.ndim - 1
    return pl.BlockSpec((None,) + arr.shape[1:], lambda *_: (layer,) + (0,) * nd,
                        pipeline_mode=pl.Buffered(1))


def _params(n_grid_dims):
    return pltpu.CompilerParams(
        dimension_semantics=("parallel",) * n_grid_dims,
        vmem_limit_bytes=V7X_VMEM_LIMIT_BYTES,
    )


def _ffn_ple_body(h_ref, p_ref, gf_ref, wg_ref, wu_ref, wd_ref, gp_ref, wpg_ref, wpp_ref, gl_ref,
                  *refs, th, final_norm, with_attn):
    o_ref = refs[-1]
    h = h_ref[0]
    if with_attn:
        a_ref, wo_ref = refs[:2]
        h = h + _dot(a_ref[0], wo_ref[...])
    xn = _rms(h, gf_ref[...]).astype(BF16)
    acc = jnp.zeros_like(h)
    for c in range(wd_ref.shape[0] // th):
        cols = slice(c * th, (c + 1) * th)
        g = _dot(xn, wg_ref[:, cols])
        u = _dot(xn, wu_ref[:, cols])
        a = (g * jax.nn.sigmoid(g) * u).astype(BF16)
        acc = acc + _dot(a, wd_ref[cols, :])
    h = h + acc
    gate = jax.nn.sigmoid(_dot(_rms(h, gp_ref[...]).astype(BF16), wpg_ref[...]))
    proj = _dot(p_ref[0].astype(BF16), wpp_ref[...])
    h = h + proj * gate
    if final_norm:
        h = _rms(h, gl_ref[...])
    o_ref[0] = h


def _ffn_ple(h, p, layer, g_ffn, w_gate, w_up, w_down, g_ple, w_pg, w_pp, g_last, attn=None, *,
             final_norm, tm, th):
    B, S, D = h.shape
    body = functools.partial(_ffn_ple_body, th=th, final_norm=final_norm, with_attn=attn is not None)
    tile = lambda w: pl.BlockSpec((1, tm, w), lambda b, i: (b, i, 0))
    in_specs = [tile(D), pl.BlockSpec((None, 1, tm, p.shape[-1]), lambda b, i: (layer, b, i, 0)),
                _layer(g_ffn, layer), _layer(w_gate, layer), _layer(w_up, layer), _layer(w_down, layer),
                _layer(g_ple, layer), _layer(w_pg, layer), _layer(w_pp, layer), _resident(g_last.shape)]
    args = [h, p, g_ffn, w_gate, w_up, w_down, g_ple, w_pg, w_pp, g_last]
    if attn is not None:
        o, w_o, j = attn
        in_specs += [tile(o.shape[-1]), _layer(w_o, j)]
        args += [o, w_o]
    return pl.pallas_call(
        body,
        grid=(B, S // tm),
        in_specs=in_specs,
        out_specs=tile(D),
        out_shape=jax.ShapeDtypeStruct((B, S, D), F32),
        compiler_params=_params(2),
        name="ffn_ple",
    )(*args)


def _qkv_body(h_ref, g_ref, w_ref, *refs, tm):
    outs, ybuf = refs[:A_GROUPS], refs[A_GROUPS]
    xn = _rms(h_ref[0], g_ref[...]).astype(BF16)
    W = A_GROUP_WIDTH
    for c in range(3 * A_GROUPS):
        part, g = divmod(c, A_GROUPS)
        d = A_DILATIONS[g]
        y = _dot(xn, w_ref[:, c * W:(c + 1) * W])
        if part == 0:
            y = y * (A_HEAD_DIM ** -0.5)
        if d == 1:
            outs[g][0, 0, :, part * W:(part + 1) * W] = y.astype(BF16)
            continue
        for s in range(W // LANES):
            ybuf[s] = y[:, s * LANES:(s + 1) * LANES]
        for r in range(d):
            for s in range(W // LANES):
                lo = part * W + s * LANES
                outs[g][0, r, :, lo:lo + LANES] = ybuf[s, pl.ds(r, tm // d, stride=d), :].astype(BF16)


def _qkv_proj(h, g, layer, w, j, *, tm):
    B, S, D = h.shape
    W3 = 3 * A_GROUP_WIDTH
    return pl.pallas_call(
        functools.partial(_qkv_body, tm=tm),
        grid=(B, S // tm),
        in_specs=[pl.BlockSpec((1, tm, D), lambda b, i: (b, i, 0)), _layer(g, layer), _layer(w, j)],
        out_specs=[pl.BlockSpec((1, d, tm // d, W3), lambda b, i: (b, 0, i, 0)) for d in A_DILATIONS],
        out_shape=[jax.ShapeDtypeStruct((B, d, S // d, W3), BF16) for d in A_DILATIONS],
        scratch_shapes=[pltpu.VMEM((A_GROUP_WIDTH // LANES, tm, LANES), F32)],
        compiler_params=_params(2),
        name="qkv_proj",
    )(h, g, w)


def _phase_major(d):
    return d % 8 == 0


def _phase_pitch(d):
    return A_SPAN // d + 8


def _attn_body(*refs):
    n_in = 5 * A_GROUPS
    bias_ref, out_ref = refs[n_in], refs[n_in + 1]
    o_bufs, l_bufs = refs[n_in + 2:n_in + 2 + A_GROUPS], refs[n_in + 2 + A_GROUPS:]
    first = pl.program_id(1) == 0
    pair = pl.program_id(2)
    head0 = lax.broadcasted_iota(jnp.int32, (1, LANES), 1) < A_HEAD_DIM
    col = lax.broadcasted_iota(jnp.int32, (1, 2 * A_BLOCK), 1)
    no_prev = jnp.where(jnp.logical_and(first, col < A_BLOCK), NEG, 0.0).astype(F32)
    ones = jnp.ones((2 * A_BLOCK, LANES), BF16)
    for g, d in enumerate(A_DILATIONS):
        q_ref, kc_ref, kp_ref, vc_ref, vp_ref = refs[5 * g:5 * g + 5]
        bias2 = bias_ref[g, pl.ds(2 * pair, 2)].reshape(2 * A_BLOCK, 2 * A_BLOCK)
        bias2_first = bias2 + no_prev
        for r in range(d):
            for qb in range(A_SPAN // d // A_BLOCK):
                q = q_ref[0, r, qb * A_BLOCK:(qb + 1) * A_BLOCK, :]
                if qb == 0:
                    keys = jnp.concatenate([kp_ref[0, r], kc_ref[0, r, :A_BLOCK, :]], axis=0)
                    vals = jnp.concatenate([vp_ref[0, r], vc_ref[0, r, :A_BLOCK, :]], axis=0)
                else:
                    keys = kc_ref[0, r, (qb - 1) * A_BLOCK:(qb + 1) * A_BLOCK, :]
                    vals = vc_ref[0, r, (qb - 1) * A_BLOCK:(qb + 1) * A_BLOCK, :]
                zero = jnp.zeros_like(q)
                q2 = jnp.concatenate([jnp.where(head0, q, zero), jnp.where(head0, zero, q)], axis=0)
                s = _dot_t(q2, keys) + (bias2_first if qb == 0 else bias2)
                m = jnp.max(s, axis=-1, keepdims=True)
                e = jnp.exp((s - m).astype(BF16))
                pvd = _dot(e, jnp.concatenate([vals, ones], axis=1))
                den = pvd[:, LANES:]
                o2 = pvd[:, :LANES] / den
                l2 = m + jnp.log(den)
                o_pair = jnp.where(head0, o2[:A_BLOCK], o2[A_BLOCK:])
                l_pair = jnp.where(head0, l2[:A_BLOCK], l2[A_BLOCK:])
                if d == 1:
                    rows = pl.ds(qb * A_BLOCK, A_BLOCK)
                elif _phase_major(d):
                    rows = pl.ds(r * _phase_pitch(d) + qb * A_BLOCK, A_BLOCK)
                else:
                    rows = pl.ds(qb * A_BLOCK * d + r, A_BLOCK, stride=d)
                o_bufs[g][rows, :] = o_pair
                l_bufs[g][rows, :] = l_pair

    def token_rows(buf, d, t0, n):
        if not _phase_major(d):
            return buf[pl.ds(t0, n), :]
        return jnp.concatenate([buf[pl.ds(step, d, stride=_phase_pitch(d)), :]
                                for step in range(t0 // d, (t0 + n) // d)], axis=0)

    chunk = 2 * A_BLOCK
    for c in range(A_SPAN // chunk):
        ls = [token_rows(l_bufs[g], d, c * chunk, chunk) for g, d in enumerate(A_DILATIONS)]
        m = functools.reduce(jnp.maximum, ls)
        es = [jnp.exp(l - m) for l in ls]
        num = sum(e * token_rows(o_bufs[g], d, c * chunk, chunk)
                  for (g, d), e in zip(enumerate(A_DILATIONS), es))
        out_ref[0, pl.ds(c * chunk, chunk), :] = (num / sum(es)).astype(BF16)


def _attention(qkv, bias):
    B, _, S, _ = qkv[0].shape
    n_pairs = A_GROUP_WIDTH // LANES
    in_specs, args = [], []
    for g, d in enumerate(A_DILATIONS):
        steps = A_SPAN // d
        bps = steps // A_BLOCK
        cur = lambda part, steps=steps, d=d: pl.BlockSpec(
            (1, d, steps, LANES), lambda b, i, pr: (b, 0, i, part * n_pairs + pr))
        prev = lambda part, bps=bps, d=d: pl.BlockSpec(
            (1, d, A_BLOCK, LANES), lambda b, i, pr: (b, 0, jnp.maximum(i * bps - 1, 0), part * n_pairs + pr))
        in_specs += [cur(0), cur(1), prev(1), cur(2), prev(2)]
        args += [qkv[g]] * 5
    buf_rows = [d * _phase_pitch(d) if _phase_major(d) else A_SPAN for d in A_DILATIONS]
    return pl.pallas_call(
        _attn_body,
        grid=(B, S // A_SPAN, n_pairs),
        in_specs=in_specs + [_resident(bias.shape)],
        out_specs=pl.BlockSpec((1, A_SPAN, LANES), lambda b, i, pr: (b, i, pr)),
        out_shape=jax.ShapeDtypeStruct((B, S, A_GROUP_WIDTH), BF16),
        scratch_shapes=[pltpu.VMEM((rows, LANES), F32) for rows in buf_rows] * 2,
        compiler_params=_params(3),
        name="attention",
    )(*args, bias)


def _t5_bucket(n):
    nf = np.maximum(n, 1).astype(np.float32)
    large = REL_MAX_EXACT + (np.log(nf / REL_MAX_EXACT) / np.log(REL_MAX_DIST / REL_MAX_EXACT)
                             * (REL_BUCKETS - REL_MAX_EXACT)).astype(np.int32)
    large = np.minimum(large, REL_BUCKETS - 1)
    return np.where(n < REL_MAX_EXACT, n, large).astype(np.int32)


def _band_buckets():
    i = np.arange(A_BLOCK)[:, None]
    j = np.arange(2 * A_BLOCK)[None, :]
    rel = A_BLOCK + i - j
    valid = (rel >= 0) & (rel <= A_BLOCK)
    return np.stack([np.where(valid, _t5_bucket(np.where(valid, rel, 0) * d), -1)
                     for d in A_DILATIONS]).astype(np.int32)


def _bias_body(table_ref, bucket_ref, o_ref):
    g = pl.program_id(0)
    bk = bucket_ref[0]
    for hd in range(A_HEADS):
        acc = jnp.full(bk.shape, NEG, F32)
        for b in range(REL_BUCKETS):
            acc = jnp.where(bk == b, table_ref[b, g * A_HEADS + hd], acc)
        o_ref[0, hd] = acc


def _band_bias(rel_table):
    buckets = jnp.asarray(_band_buckets())
    G, Q, Kk = buckets.shape
    return pl.pallas_call(
        _bias_body,
        grid=(G,),
        in_specs=[pl.BlockSpec(memory_space=pltpu.SMEM),
                  pl.BlockSpec((1, Q, Kk), lambda g: (g, 0, 0))],
        out_specs=pl.BlockSpec((1, A_HEADS, Q, Kk), lambda g: (g, 0, 0, 0)),
        out_shape=jax.ShapeDtypeStruct((G, A_HEADS, Q, Kk), F32),
        compiler_params=_params(1),
        name="band_bias",
    )(rel_table, buckets)


def _mixer_dilated(h, g_mix, layer, w_qkv, j, bias, *, tm):
    return _attention(_qkv_proj(h, g_mix, layer, w_qkv, j, tm=tm), bias)


def _pool_body(h_ref, halo_ref, g_ref, win_ref, wgrp_ref, sc_ref, wout_ref, o_ref, *, tm):
    i = pl.program_id(1)
    h = h_ref[0]
    gain = g_ref[...]
    y_t = _dot(_rms(h, gain).astype(BF16), win_ref[...])
    y_h = _dot(_rms(halo_ref[0], gain).astype(BF16), win_ref[...])
    y_h = jnp.where(i == 0, 0.0, y_h)
    y = jnp.concatenate([y_h, y_t], axis=0)
    pos = (lax.broadcasted_iota(jnp.int32, (tm, 1), 0) + (i * tm + 1)).astype(F32)
    gw = y.shape[1] // len(B_WINDOWS)
    zs = []
    for g, win in enumerate(B_WINDOWS):
        yg = y[:, g * gw:(g + 1) * gw]
        s = yg
        sh = 1
        while sh < win:
            s = s + pltpu.roll(s, sh, axis=0)
            sh *= 2
        mean = s[B_HALO:] * (1.0 / jnp.minimum(pos, float(win)))
        pooled = mean - yg[B_HALO:]
        zs.append(_dot(pooled.astype(BF16), wgrp_ref[g]))
    z = jnp.concatenate(zs, axis=1) * sc_ref[...]
    o_ref[0] = h + _dot(z.astype(BF16), wout_ref[...])


def _mixer_pool(h, g_mix, layer, w_in, w_grp, scale, w_out, j, *, tm):
    B, S, D = h.shape
    hpt = tm // B_HALO
    return pl.pallas_call(
        functools.partial(_pool_body, tm=tm),
        grid=(B, S // tm),
        in_specs=[pl.BlockSpec((1, tm, D), lambda b, i: (b, i, 0)),
                  pl.BlockSpec((1, B_HALO, D), lambda b, i: (b, jnp.maximum(i * hpt - 1, 0), 0)),
                  _layer(g_mix, layer), _layer(w_in, j), _layer(w_grp, j), _layer(scale, j), _layer(w_out, j)],
        out_specs=pl.BlockSpec((1, tm, D), lambda b, i: (b, i, 0)),
        out_shape=jax.ShapeDtypeStruct((B, S, D), F32),
        compiler_params=_params(2),
        name="mixer_pool",
    )(h, h, g_mix, w_in, w_grp, scale, w_out)


def _sgu_body(h_ref, g_ref, win_ref, vg_ref, ws_ref, bs_ref, wout_ref, o_ref, *, tm):
    h = h_ref[0]
    D = h.shape[1]
    xn = _rms(h, g_ref[...]).astype(BF16)
    u = jax.nn.gelu(_dot(xn, win_ref[:, :D]))
    v = jax.nn.gelu(_dot(xn, win_ref[:, D:]))
    mu = jnp.mean(v, axis=-1, keepdims=True)
    var = jnp.mean(jnp.square(v - mu), axis=-1, keepdims=True)
    vn = ((v - mu) * lax.rsqrt(var + EPS) * vg_ref[...]).astype(BF16)
    t = lax.broadcasted_iota(jnp.int32, (C_CHUNK, C_CHUNK), 0)
    s = lax.broadcasted_iota(jnp.int32, (C_CHUNK, C_CHUNK), 1)
    gw = D // C_GROUPS
    cols = []
    for g in range(C_GROUPS):
        wm = jnp.where(s <= t, ws_ref[g], 0.0).astype(BF16)
        b = bs_ref[:, g:g + 1]
        rows = []
        for n in range(tm // C_CHUNK):
            vc = vn[n * C_CHUNK:(n + 1) * C_CHUNK, g * gw:(g + 1) * gw]
            rows.append(_dot(wm, vc) + b)
        cols.append(jnp.concatenate(rows, axis=0))
    sp = jnp.concatenate(cols, axis=1)
    o_ref[0] = h + _dot((u * sp).astype(BF16), wout_ref[...])


def _mixer_sgu(h, g_mix, layer, w_in, v_gain, w_s, b_s_t, w_out, j, *, tm):
    B, S, D = h.shape
    return pl.pallas_call(
        functools.partial(_sgu_body, tm=tm),
        grid=(B, S // tm),
        in_specs=[pl.BlockSpec((1, tm, D), lambda b, i: (b, i, 0)),
                  _layer(g_mix, layer), _layer(w_in, j), _layer(v_gain, j),
                  _layer(w_s, j), _layer(b_s_t, j), _layer(w_out, j)],
        out_specs=pl.BlockSpec((1, tm, D), lambda b, i: (b, i, 0)),
        out_shape=jax.ShapeDtypeStruct((B, S, D), F32),
        compiler_params=_params(2),
        name="mixer_sgu",
    )(h, g_mix, w_in, v_gain, w_s, b_s_t, w_out)


def kernel(x, p, rel_table, norm_mix, norm_ffn, norm_ple, final_norm, a_w_qkv, a_w_o, b_w_in, b_w_grp, b_scale, b_w_out, c_w_in, c_v_gain, c_w_s, c_b_s, c_w_out, ffn_w_gate, ffn_w_up, ffn_w_down, ple_w_gate, ple_w_proj):
    depth = norm_mix.shape[0]
    rows = lambda v: v[..., None, :]
    bf = lambda w: w.astype(BF16)
    tm = 512
    g_mix, g_ffn, g_ple, g_last = rows(norm_mix), rows(norm_ffn), rows(norm_ple), rows(final_norm)
    a_w_qkv, a_w_o = bf(a_w_qkv), bf(a_w_o)
    b_w_in, b_w_grp, b_w_out = bf(b_w_in), bf(b_w_grp), bf(b_w_out)
    c_w_in, c_w_out = bf(c_w_in), bf(c_w_out)
    ffn_w_gate, ffn_w_up, ffn_w_down = bf(ffn_w_gate), bf(ffn_w_up), bf(ffn_w_down)
    ple_w_gate, ple_w_proj = bf(ple_w_gate), bf(ple_w_proj)
    bias = _band_bias(rel_table)
    h = x
    for i in range(depth):
        kind, j = i % 3, i // 3
        attn = None
        if kind == 0:
            attn = (_mixer_dilated(h, g_mix, i, a_w_qkv, j, bias, tm=tm), a_w_o, j)
        elif kind == 1:
            h = _mixer_pool(h, g_mix, i, b_w_in, b_w_grp, rows(b_scale), b_w_out, j, tm=tm)
        else:
            h = _mixer_sgu(h, g_mix, i, c_w_in, rows(c_v_gain), c_w_s, jnp.swapaxes(c_b_s, 1, 2), c_w_out, j,
                           tm=tm)
        h = _ffn_ple(h, p, i, g_ffn, ffn_w_gate, ffn_w_up, ffn_w_down, g_ple, ple_w_gate, ple_w_proj,
                     g_last, attn, final_norm=(i == depth - 1), tm=tm, th=256)
    return h
```

```python
import functools

import jax
import jax.numpy as jnp
import numpy as np
from jax import lax
from jax.experimental import pallas as pl
from jax.experimental.pallas import tpu as pltpu

EPS = 1e-6
NEG = -1e30

A_DILATIONS = (1, 4, 16)
A_GROUPS = 3
A_HEADS = 8
A_HEAD_DIM = 64
A_GROUP_WIDTH = A_HEADS * A_HEAD_DIM
A_BLOCK = 128
A_SPAN = A_BLOCK * max(A_DILATIONS)
REL_BUCKETS = 32
REL_MAX_EXACT = REL_BUCKETS // 2
REL_MAX_DIST = 2048
B_WINDOWS = (2, 4, 8, 16)
B_HALO = 16
C_CHUNK = 128
C_GROUPS = 4

LANES = 128
V7X_VMEM_LIMIT_BYTES = 56 * 1024 * 1024

BF16 = jnp.bfloat16
F32 = jnp.float32


def _dot(a, b):
    return jnp.dot(a, b, preferred_element_type=F32)


def _dot_t(a, b):
    return lax.dot_general(a, b, (((1,), (1,)), ((), ())), preferred_element_type=F32)


def _rms(x, g):
    ms = jnp.mean(x * x, axis=-1, keepdims=True)
    return x * lax.rsqrt(ms + EPS) * g


def _resident(shape):
    nd = len(shape)
    return pl.BlockSpec(shape, lambda *_: (0,) * nd, pipeline_mode=pl.Buffered(1))


def _layer(arr, layer):
    nd = arr.ndim - 1
    return pl.BlockSpec((None,) + arr.shape[1:], lambda *_: (layer,) + (0,) * nd,
                        pipeline_mode=pl.Buffered(1))


def _params(n_grid_dims):
    return pltpu.CompilerParams(
        dimension_semantics=("parallel",) * n_grid_dims,
        vmem_limit_bytes=V7X_VMEM_LIMIT_BYTES,
    )


def _ffn_ple_body(h_ref, p_ref, gf_ref, wg_ref, wu_ref, wd_ref, gp_ref, wpg_ref, wpp_ref, gl_ref,
                  *refs, th, n_sub, final_norm, with_attn):
    o_ref = refs[-1]
    ts = h_ref.shape[1] // n_sub
    subs = [slice(k * ts, (k + 1) * ts) for k in range(n_sub)]
    hs = [h_ref[0, rows, :] for rows in subs]
    if with_attn:
        a_ref, wo_ref = refs[:2]
        hs = [h + _dot(a_ref[0, rows, :], wo_ref[...]) for h, rows in zip(hs, subs)]
    xns = [_rms(h, gf_ref[...]).astype(BF16) for h in hs]
    accs = [jnp.zeros_like(h) for h in hs]
    for c in range(wd_ref.shape[0] // th):
        cols = slice(c * th, (c + 1) * th)
        for k, xn in enumerate(xns):
            g = _dot(xn, wg_ref[:, cols])
            u = _dot(xn, wu_ref[:, cols])
            a = (g * jax.nn.sigmoid(g) * u).astype(BF16)
            accs[k] = accs[k] + _dot(a, wd_ref[cols, :])
    for h, acc, rows in zip(hs, accs, subs):
        h = h + acc
        gate = jax.nn.sigmoid(_dot(_rms(h, gp_ref[...]).astype(BF16), wpg_ref[...]))
        proj = _dot(p_ref[0, rows, :].astype(BF16), wpp_ref[...])
        h = h + proj * gate
        if final_norm:
            h = _rms(h, gl_ref[...])
        o_ref[0, rows, :] = h


def _ffn_ple(h, p, layer, g_ffn, w_gate, w_up, w_down, g_ple, w_pg, w_pp, g_last, attn=None, *,
             final_norm, tm, th, n_sub):
    B, S, D = h.shape
    body = functools.partial(_ffn_ple_body, th=th, n_sub=n_sub, final_norm=final_norm,
                             with_attn=attn is not None)
    tile = lambda w: pl.BlockSpec((1, tm, w), lambda b, i: (b, i, 0))
    in_specs = [tile(D), pl.BlockSpec((None, 1, tm, p.shape[-1]), lambda b, i: (layer, b, i, 0)),
                _layer(g_ffn, layer), _layer(w_gate, layer), _layer(w_up, layer), _layer(w_down, layer),
                _layer(g_ple, layer), _layer(w_pg, layer), _layer(w_pp, layer), _resident(g_last.shape)]
    args = [h, p, g_ffn, w_gate, w_up, w_down, g_ple, w_pg, w_pp, g_last]
    if attn is not None:
        o, w_o, j = attn
        in_specs += [tile(o.shape[-1]), _layer(w_o, j)]
        args += [o, w_o]
    return pl.pallas_call(
        body,
        grid=(B, S // tm),
        in_specs=in_specs,
        out_specs=tile(D),
        out_shape=jax.ShapeDtypeStruct((B, S, D), F32),
        compiler_params=_params(2),
        name="ffn_ple",
    )(*args)


def _qkv_body(h_ref, g_ref, w_ref, *refs, tm):
    outs, xbuf = refs[:A_GROUPS], refs[A_GROUPS]
    xn = _rms(h_ref[0], g_ref[...])
    n_slab = xn.shape[1] // LANES
    for s in range(n_slab):
        xbuf[s] = xn[:, s * LANES:(s + 1) * LANES]
    W = A_GROUP_WIDTH
    for g, d in enumerate(A_DILATIONS):
        if d == 1:
            lhs = xn.astype(BF16)
        else:
            lhs = jnp.concatenate(
                [jnp.concatenate([xbuf[s, pl.ds(r, tm // d, stride=d), :] for s in range(n_slab)], axis=1)
                 for r in range(d)], axis=0).astype(BF16)
        for part in range(3):
            c = part * A_GROUPS + g
            y = _dot(lhs, w_ref[:, c * W:(c + 1) * W])
            if part == 0:
                y = y * (A_HEAD_DIM ** -0.5)
            y = y.astype(BF16)
            for r in range(d):
                outs[g][0, r, :, part * W:(part + 1) * W] = y[r * (tm // d):(r + 1) * (tm // d), :]


def _qkv_proj(h, g, layer, w, j, *, tm):
    B, S, D = h.shape
    W3 = 3 * A_GROUP_WIDTH
    return pl.pallas_call(
        functools.partial(_qkv_body, tm=tm),
        grid=(B, S // tm),
        in_specs=[pl.BlockSpec((1, tm, D), lambda b, i: (b, i, 0)), _layer(g, layer), _layer(w, j)],
        out_specs=[pl.BlockSpec((1, d, tm // d, W3), lambda b, i: (b, 0, i, 0)) for d in A_DILATIONS],
        out_shape=[jax.ShapeDtypeStruct((B, d, S // d, W3), BF16) for d in A_DILATIONS],
        scratch_shapes=[pltpu.VMEM((D // LANES, tm, LANES), F32)],
        compiler_params=_params(2),
        name="qkv_proj",
    )(h, g, w)


def _phase_major(d):
    return d % 8 == 0


def _phase_pitch(d):
    return A_SPAN // d + 8


def _attn_body(*refs):
    n_in = 5 * A_GROUPS
    bias_ref, out_ref = refs[n_in], refs[n_in + 1]
    o_bufs, l_bufs = refs[n_in + 2:n_in + 2 + A_GROUPS], refs[n_in + 2 + A_GROUPS:]
    first = pl.program_id(1) == 0
    pair = pl.program_id(2)
    head0 = lax.broadcasted_iota(jnp.int32, (1, LANES), 1) < A_HEAD_DIM
    col = lax.broadcasted_iota(jnp.int32, (1, 2 * A_BLOCK), 1)
    no_prev = jnp.where(jnp.logical_and(first, col < A_BLOCK), NEG, 0.0).astype(F32)
    ones = jnp.ones((2 * A_BLOCK, LANES), BF16)
    for g, d in enumerate(A_DILATIONS):
        q_ref, kc_ref, kp_ref, vc_ref, vp_ref = refs[5 * g:5 * g + 5]
        bias2 = bias_ref[g, pl.ds(2 * pair, 2)].reshape(2 * A_BLOCK, 2 * A_BLOCK)
        bias2_first = bias2 + no_prev
        for r in range(d):
            for qb in range(A_SPAN // d // A_BLOCK):
                q = q_ref[0, r, qb * A_BLOCK:(qb + 1) * A_BLOCK, :]
                if qb == 0:
                    keys = jnp.concatenate([kp_ref[0, r], kc_ref[0, r, :A_BLOCK, :]], axis=0)
                    vals = jnp.concatenate([vp_ref[0, r], vc_ref[0, r, :A_BLOCK, :]], axis=0)
                else:
                    keys = kc_ref[0, r, (qb - 1) * A_BLOCK:(qb + 1) * A_BLOCK, :]
                    vals = vc_ref[0, r, (qb - 1) * A_BLOCK:(qb + 1) * A_BLOCK, :]
                zero = jnp.zeros_like(q)
                q2 = jnp.concatenate([jnp.where(head0, q, zero), jnp.where(head0, zero, q)], axis=0)
                s = _dot_t(q2, keys) + (bias2_first if qb == 0 else bias2)
                m = jnp.max(s, axis=-1, keepdims=True)
                e = jnp.exp((s - m).astype(BF16))
                pvd = _dot(e, jnp.concatenate([vals, ones], axis=1))
                den = pvd[:, LANES:]
                o2 = pvd[:, :LANES] / den
                l2 = m + jnp.log(den)
                o_pair = jnp.where(head0, o2[:A_BLOCK], o2[A_BLOCK:])
                l_pair = jnp.where(head0, l2[:A_BLOCK], l2[A_BLOCK:])
                if d == 1:
                    rows = pl.ds(qb * A_BLOCK, A_BLOCK)
                elif _phase_major(d):
                    rows = pl.ds(r * _phase_pitch(d) + qb * A_BLOCK, A_BLOCK)
                else:
                    rows = pl.ds(qb * A_BLOCK * d + r, A_BLOCK, stride=d)
                o_bufs[g][rows, :] = o_pair
                l_bufs[g][rows, :] = l_pair

    def token_rows(buf, d, t0, n):
        if not _phase_major(d):
            return buf[pl.ds(t0, n), :]
        return jnp.concatenate([buf[pl.ds(step, d, stride=_phase_pitch(d)), :]
                                for step in range(t0 // d, (t0 + n) // d)], axis=0)

    chunk = 2 * A_BLOCK
    for c in range(A_SPAN // chunk):
        ls = [token_rows(l_bufs[g], d, c * chunk, chunk) for g, d in enumerate(A_DILATIONS)]
        m = functools.reduce(jnp.maximum, ls)
        es = [jnp.exp(l - m) for l in ls]
        num = sum(e * token_rows(o_bufs[g], d, c * chunk, chunk)
                  for (g, d), e in zip(enumerate(A_DILATIONS), es))
        out_ref[0, pl.ds(c * chunk, chunk), :] = (num / sum(es)).astype(BF16)


def _attention(qkv, bias):
    B, _, S, _ = qkv[0].shape
    n_pairs = A_GROUP_WIDTH // LANES
    in_specs, args = [], []
    for g, d in enumerate(A_DILATIONS):
        steps = A_SPAN // d
        bps = steps // A_BLOCK
        cur = lambda part, steps=steps, d=d: pl.BlockSpec(
            (1, d, steps, LANES), lambda b, i, pr: (b, 0, i, part * n_pairs + pr))
        prev = lambda part, bps=bps, d=d: pl.BlockSpec(
            (1, d, A_BLOCK, LANES), lambda b, i, pr: (b, 0, jnp.maximum(i * bps - 1, 0), part * n_pairs + pr))
        in_specs += [cur(0), cur(1), prev(1), cur(2), prev(2)]
        args += [qkv[g]] * 5
    buf_rows = [d * _phase_pitch(d) if _phase_major(d) else A_SPAN for d in A_DILATIONS]
    return pl.pallas_call(
        _attn_body,
        grid=(B, S // A_SPAN, n_pairs),
        in_specs=in_specs + [_resident(bias.shape)],
        out_specs=pl.BlockSpec((1, A_SPAN, LANES), lambda b, i, pr: (b, i, pr)),
        out_shape=jax.ShapeDtypeStruct((B, S, A_GROUP_WIDTH), BF16),
        scratch_shapes=[pltpu.VMEM((rows, LANES), F32) for rows in buf_rows] * 2,
        compiler_params=_params(3),
        name="attention",
    )(*args, bias)


def _t5_bucket(n):
    nf = np.maximum(n, 1).astype(np.float32)
    large = REL_MAX_EXACT + (np.log(nf / REL_MAX_EXACT) / np.log(REL_MAX_DIST / REL_MAX_EXACT)
                             * (REL_BUCKETS - REL_MAX_EXACT)).astype(np.int32)
    large = np.minimum(large, REL_BUCKETS - 1)
    return np.where(n < REL_MAX_EXACT, n, large).astype(np.int32)


def _band_buckets():
    i = np.arange(A_BLOCK)[:, None]
    j = np.arange(2 * A_BLOCK)[None, :]
    rel = A_BLOCK + i - j
    valid = (rel >= 0) & (rel <= A_BLOCK)
    return np.stack([np.where(valid, _t5_bucket(np.where(valid, rel, 0) * d), -1)
                     for d in A_DILATIONS]).astype(np.int32)


def _bias_body(table_ref, bucket_ref, o_ref):
    g = pl.program_id(0)
    bk = bucket_ref[0]
    for hd in range(A_HEADS):
        acc = jnp.full(bk.shape, NEG, F32)
        for b in range(REL_BUCKETS):
            acc = jnp.where(bk == b, table_ref[b, g * A_HEADS + hd], acc)
        o_ref[0, hd] = acc


def _band_bias(rel_table):
    buckets = jnp.asarray(_band_buckets())
    G, Q, Kk = buckets.shape
    return pl.pallas_call(
        _bias_body,
        grid=(G,),
        in_specs=[pl.BlockSpec(memory_space=pltpu.SMEM),
                  pl.BlockSpec((1, Q, Kk), lambda g: (g, 0, 0))],
        out_specs=pl.BlockSpec((1, A_HEADS, Q, Kk), lambda g: (g, 0, 0, 0)),
        out_shape=jax.ShapeDtypeStruct((G, A_HEADS, Q, Kk), F32),
        compiler_params=_params(1),
        name="band_bias",
    )(rel_table, buckets)


def _mixer_dilated(h, g_mix, layer, w_qkv, j, bias, *, tm):
    return _attention(_qkv_proj(h, g_mix, layer, w_qkv, j, tm=tm), bias)


def _pool_body(h_ref, halo_ref, g_ref, win_ref, wgrp_ref, sc_ref, wout_ref, o_ref, *, tm, n_sub):
    i = pl.program_id(1)
    gain = g_ref[...]
    ts = tm // n_sub
    gw = h_ref.shape[2] // len(B_WINDOWS)
    xn = jnp.concatenate([_rms(halo_ref[0], gain), _rms(h_ref[0], gain)], axis=0).astype(BF16)
    ys = []
    for k in range(n_sub):
        y = _dot(xn[k * ts:(k + 1) * ts + B_HALO], win_ref[...])
        if k == 0:
            y = jnp.concatenate([jnp.where(i == 0, 0.0, y[:B_HALO]), y[B_HALO:]], axis=0)
        ys.append(y)
    pooled = []
    for k, y in enumerate(ys):
        pos = (lax.broadcasted_iota(jnp.int32, (ts, 1), 0) + (i * tm + k * ts + 1)).astype(F32)
        groups = []
        for g, win in enumerate(B_WINDOWS):
            yg = y[:, g * gw:(g + 1) * gw]
            s = yg
            sh = 1
            while sh < win:
                s = s + pltpu.roll(s, sh, axis=0)
                sh *= 2
            mean = s[B_HALO:] * (1.0 / jnp.minimum(pos, float(win)))
            groups.append((mean - yg[B_HALO:]).astype(BF16))
        pooled.append(groups)
    for k, groups in enumerate(pooled):
        z = jnp.concatenate([_dot(pg, wgrp_ref[g]) for g, pg in enumerate(groups)], axis=1) * sc_ref[...]
        rows = slice(k * ts, (k + 1) * ts)
        o_ref[0, rows, :] = h_ref[0, rows, :] + _dot(z.astype(BF16), wout_ref[...])


def _mixer_pool(h, g_mix, layer, w_in, w_grp, scale, w_out, j, *, tm, n_sub):
    B, S, D = h.shape
    hpt = tm // B_HALO
    return pl.pallas_call(
        functools.partial(_pool_body, tm=tm, n_sub=n_sub),
        grid=(B, S // tm),
        in_specs=[pl.BlockSpec((1, tm, D), lambda b, i: (b, i, 0)),
                  pl.BlockSpec((1, B_HALO, D), lambda b, i: (b, jnp.maximum(i * hpt - 1, 0), 0)),
                  _layer(g_mix, layer), _layer(w_in, j), _layer(w_grp, j), _layer(scale, j), _layer(w_out, j)],
        out_specs=pl.BlockSpec((1, tm, D), lambda b, i: (b, i, 0)),
        out_shape=jax.ShapeDtypeStruct((B, S, D), F32),
        compiler_params=_params(2),
        name="mixer_pool",
    )(h, h, g_mix, w_in, w_grp, scale, w_out)


def _sgu_body(h_ref, g_ref, win_ref, vg_ref, ws_ref, bs_ref, wout_ref, o_ref, *, tm, n_sub):
    D = h_ref.shape[2]
    gw = D // C_GROUPS
    t = lax.broadcasted_iota(jnp.int32, (C_CHUNK, C_CHUNK), 0)
    s = lax.broadcasted_iota(jnp.int32, (C_CHUNK, C_CHUNK), 1)
    wms = [jnp.where(s <= t, ws_ref[g], 0.0).astype(BF16) for g in range(C_GROUPS)]
    ts = tm // n_sub
    subs = [slice(k * ts, (k + 1) * ts) for k in range(n_sub)]
    xns = [_rms(h_ref[0, rows, :], g_ref[...]).astype(BF16) for rows in subs]
    zs = [(_dot(xn, win_ref[:, :D]), _dot(xn, win_ref[:, D:])) for xn in xns]
    uvs = []
    for zu, zv in zs:
        v = jax.nn.gelu(zv)
        mu = jnp.mean(v, axis=-1, keepdims=True)
        var = jnp.mean(jnp.square(v - mu), axis=-1, keepdims=True)
        vn = ((v - mu) * lax.rsqrt(var + EPS) * vg_ref[...]).astype(BF16)
        uvs.append((jax.nn.gelu(zu), vn))
    for (u, vn), rows in zip(uvs, subs):
        cols = []
        for g in range(C_GROUPS):
            b = bs_ref[:, g:g + 1]
            cols.append(jnp.concatenate(
                [_dot(wms[g], vn[n * C_CHUNK:(n + 1) * C_CHUNK, g * gw:(g + 1) * gw]) + b
                 for n in range(ts // C_CHUNK)], axis=0))
        sp = jnp.concatenate(cols, axis=1)
        o_ref[0, rows, :] = h_ref[0, rows, :] + _dot((u * sp).astype(BF16), wout_ref[...])


def _mixer_sgu(h, g_mix, layer, w_in, v_gain, w_s, b_s_t, w_out, j, *, tm, n_sub):
    B, S, D = h.shape
    return pl.pallas_call(
        functools.partial(_sgu_body, tm=tm, n_sub=n_sub),
        grid=(B, S // tm),
        in_specs=[pl.BlockSpec((1, tm, D), lambda b, i: (b, i, 0)),
                  _layer(g_mix, layer), _layer(w_in, j), _layer(v_gain, j),
                  _layer(w_s, j), _layer(b_s_t, j), _layer(w_out, j)],
        out_specs=pl.BlockSpec((1, tm, D), lambda b, i: (b, i, 0)),
        out_shape=jax.ShapeDtypeStruct((B, S, D), F32),
        compiler_params=_params(2),
        name="mixer_sgu",
    )(h, g_mix, w_in, v_gain, w_s, b_s_t, w_out)


def kernel(x, p, rel_table, norm_mix, norm_ffn, norm_ple, final_norm, a_w_qkv, a_w_o, b_w_in, b_w_grp, b_scale, b_w_out, c_w_in, c_v_gain, c_w_s, c_b_s, c_w_out, ffn_w_gate, ffn_w_up, ffn_w_down, ple_w_gate, ple_w_proj):
    depth = norm_mix.shape[0]
    rows = lambda v: v[..., None, :]
    bf = lambda w: w.astype(BF16)
    tm = 512
    g_mix, g_ffn, g_ple, g_last = rows(norm_mix), rows(norm_ffn), rows(norm_ple), rows(final_norm)
    a_w_qkv, a_w_o = bf(a_w_qkv), bf(a_w_o)
    b_w_in, b_w_grp, b_w_out = bf(b_w_in), bf(b_w_grp), bf(b_w_out)
    c_w_in, c_w_out = bf(c_w_in), bf(c_w_out)
    ffn_w_gate, ffn_w_up, ffn_w_down = bf(ffn_w_gate), bf(ffn_w_up), bf(ffn_w_down)
    ple_w_gate, ple_w_proj = bf(ple_w_gate), bf(ple_w_proj)
    bias = _band_bias(rel_table)
    h = x
    for i in range(depth):
        kind, j = i % 3, i // 3
        attn = None
        if kind == 0:
            attn = (_mixer_dilated(h, g_mix, i, a_w_qkv, j, bias, tm=tm), a_w_o, j)
        elif kind == 1:
            h = _mixer_pool(h, g_mix, i, b_w_in, b_w_grp, rows(b_scale), b_w_out, j, tm=1024, n_sub=2)
        else:
            h = _mixer_sgu(h, g_mix, i, c_w_in, rows(c_v_gain), c_w_s, jnp.swapaxes(c_b_s, 1, 2), c_w_out, j,
                           tm=1024, n_sub=2)
        h = _ffn_ple(h, p, i, g_ffn, ffn_w_gate, ffn_w_up, ffn_w_down, g_ple, ple_w_gate, ple_w_proj,
                     g_last, attn, final_norm=(i == depth - 1), tm=1024, th=256, n_sub=2)
    return h
```

```python
import functools

import jax
import jax.numpy as jnp
import numpy as np
from jax import lax
from jax.experimental import pallas as pl
from jax.experimental.pallas import tpu as pltpu

EPS = 1e-6
NEG = -1e30

A_DILATIONS = (1, 4, 16)
A_GROUPS = 3
A_HEADS = 8
A_HEAD_DIM = 64
A_GROUP_WIDTH = A_HEADS * A_HEAD_DIM
A_BLOCK = 128
A_SPAN = A_BLOCK * max(A_DILATIONS)
REL_BUCKETS = 32
REL_MAX_EXACT = REL_BUCKETS // 2
REL_MAX_DIST = 2048
B_WINDOWS = (2, 4, 8, 16)
B_HALO = 16
C_CHUNK = 128
C_GROUPS = 4

LANES = 128
V7X_VMEM_LIMIT_BYTES = 56 * 1024 * 1024

BF16 = jnp.bfloat16
F32 = jnp.float32


def _dot(a, b):
    return jnp.dot(a, b, preferred_element_type=F32)


def _dot_t(a, b):
    return lax.dot_general(a, b, (((1,), (1,)), ((), ())), preferred_element_type=F32)


def _rms(x, g):
    ms = jnp.mean(x * x, axis=-1, keepdims=True)
    return x * lax.rsqrt(ms + EPS) * g


def _resident(shape):
    nd = len(shape)
    return pl.BlockSpec(shape, lambda *_: (0,) * nd, pipeline_mode=pl.Buffered(1))


def _layer(arr, layer):
    nd = arr.ndim - 1
    return pl.BlockSpec((None,) + arr.shape[1:], lambda *_: (layer,) + (0,) * nd,
                        pipeline_mode=pl.Buffered(1))


def _params(n_grid_dims):
    return pltpu.CompilerParams(
        dimension_semantics=("parallel",) * n_grid_dims,
        vmem_limit_bytes=V7X_VMEM_LIMIT_BYTES,
    )


def _ffn_ple_body(h_ref, p_ref, gf_ref, wg_ref, wu_ref, wd_ref, gp_ref, wpg_ref, wpp_ref, gl_ref,
                  *refs, th, n_sub, final_norm, with_attn):
    o_ref = refs[-1]
    ts = h_ref.shape[1] // n_sub
    subs = [slice(k * ts, (k + 1) * ts) for k in range(n_sub)]
    hs = [h_ref[0, rows, :] for rows in subs]
    if with_attn:
        a_ref, wo_ref = refs[:2]
        hs = [h + _dot(a_ref[0, rows, :], wo_ref[...]) for h, rows in zip(hs, subs)]
    xns = [_rms(h, gf_ref[...]).astype(BF16) for h in hs]
    accs = [jnp.zeros_like(h) for h in hs]
    for c in range(wd_ref.shape[0] // th):
        cols = slice(c * th, (c + 1) * th)
        for k, xn in enumerate(xns):
            g = _dot(xn, wg_ref[:, cols])
            u = _dot(xn, wu_ref[:, cols])
            a = (g * jax.nn.sigmoid(g) * u).astype(BF16)
            accs[k] = accs[k] + _dot(a, wd_ref[cols, :])
    for h, acc, rows in zip(hs, accs, subs):
        h = h + acc
        gate = jax.nn.sigmoid(_dot(_rms(h, gp_ref[...]).astype(BF16), wpg_ref[...]))
        proj = _dot(p_ref[0, rows, :].astype(BF16), wpp_ref[...])
        h = h + proj * gate
        if final_norm:
            h = _rms(h, gl_ref[...])
        o_ref[0, rows, :] = h


def _ffn_ple(h, p, layer, g_ffn, w_gate, w_up, w_down, g_ple, w_pg, w_pp, g_last, attn=None, *,
             final_norm, tm, th, n_sub):
    B, S, D = h.shape
    body = functools.partial(_ffn_ple_body, th=th, n_sub=n_sub, final_norm=final_norm,
                             with_attn=attn is not None)
    tile = lambda w: pl.BlockSpec((1, tm, w), lambda b, i: (b, i, 0))
    in_specs = [tile(D), pl.BlockSpec((None, 1, tm, p.shape[-1]), lambda b, i: (layer, b, i, 0)),
                _layer(g_ffn, layer), _layer(w_gate, layer), _layer(w_up, layer), _layer(w_down, layer),
                _layer(g_ple, layer), _layer(w_pg, layer), _layer(w_pp, layer), _resident(g_last.shape)]
    args = [h, p, g_ffn, w_gate, w_up, w_down, g_ple, w_pg, w_pp, g_last]
    if attn is not None:
        o, w_o, j = attn
        in_specs += [tile(o.shape[-1]), _layer(w_o, j)]
        args += [o, w_o]
    return pl.pallas_call(
        body,
        grid=(B, S // tm),
        in_specs=in_specs,
        out_specs=tile(D),
        out_shape=jax.ShapeDtypeStruct((B, S, D), F32),
        compiler_params=_params(2),
        name="ffn_ple",
    )(*args)


def _qkv_body(h_ref, g_ref, w_ref, *refs, tm, n_sub):
    outs, xbuf = refs[:A_GROUPS], refs[A_GROUPS]
    W = A_GROUP_WIDTH
    n_slab = h_ref.shape[2] // LANES
    ts = tm // n_sub
    xns = []
    for k in range(n_sub):
        xn = _rms(h_ref[0, k * ts:(k + 1) * ts, :], g_ref[...])
        for s in range(n_slab):
            xbuf[k, s] = xn[:, s * LANES:(s + 1) * LANES]
        xns.append(xn.astype(BF16))
    for g, d in enumerate(A_DILATIONS):
        for k in range(n_sub):
            if d == 1:
                lhs = xns[k]
            else:
                lhs = jnp.concatenate(
                    [jnp.concatenate([xbuf[k, s, pl.ds(r, ts // d, stride=d), :] for s in range(n_slab)],
                                     axis=1) for r in range(d)], axis=0).astype(BF16)
            for part in range(3):
                c = part * A_GROUPS + g
                y = _dot(lhs, w_ref[:, c * W:(c + 1) * W])
                if part == 0:
                    y = y * (A_HEAD_DIM ** -0.5)
                y = y.astype(BF16)
                for r in range(d):
                    rows_in = slice(r * (ts // d), (r + 1) * (ts // d))
                    rows_out = slice(k * (ts // d), (k + 1) * (ts // d))
                    for pr in range(W // LANES):
                        lo = (3 * pr + part) * LANES
                        outs[g][0, r, rows_out, lo:lo + LANES] = y[rows_in, pr * LANES:(pr + 1) * LANES]


def _qkv_proj(h, g, layer, w, j, *, tm, n_sub):
    B, S, D = h.shape
    W3 = 3 * A_GROUP_WIDTH
    return pl.pallas_call(
        functools.partial(_qkv_body, tm=tm, n_sub=n_sub),
        grid=(B, S // tm),
        in_specs=[pl.BlockSpec((1, tm, D), lambda b, i: (b, i, 0)), _layer(g, layer), _layer(w, j)],
        out_specs=[pl.BlockSpec((1, d, tm // d, W3), lambda b, i: (b, 0, i, 0)) for d in A_DILATIONS],
        out_shape=[jax.ShapeDtypeStruct((B, d, S // d, W3), BF16) for d in A_DILATIONS],
        scratch_shapes=[pltpu.VMEM((n_sub, D // LANES, tm // n_sub, LANES), F32)],
        compiler_params=_params(2),
        name="qkv_proj",
    )(h, g, w)


def _phase_major(d):
    return d % 8 == 0


def _phase_pitch(d):
    return A_SPAN // d + 8


def _attn_body(*refs):
    n_in = 2 * A_GROUPS
    bias_ref, out_ref = refs[n_in], refs[n_in + 1]
    bufs = refs[n_in + 2:]
    pv_bufs, m_bufs, den_bufs = bufs[:A_GROUPS], bufs[A_GROUPS:2 * A_GROUPS], bufs[2 * A_GROUPS:]
    first = pl.program_id(1) == 0
    pair = pl.program_id(2)
    head0 = lax.broadcasted_iota(jnp.int32, (1, LANES), 1) < A_HEAD_DIM
    col = lax.broadcasted_iota(jnp.int32, (1, 2 * A_BLOCK), 1)
    no_prev = jnp.where(jnp.logical_and(first, col < A_BLOCK), NEG, 0.0).astype(F32)
    ones = jnp.ones((2 * A_BLOCK, LANES), BF16)
    q_cols, k_cols, v_cols = (slice(part * LANES, (part + 1) * LANES) for part in range(3))
    for g, d in enumerate(A_DILATIONS):
        cur_ref, prev_ref = refs[2 * g:2 * g + 2]
        bias2 = bias_ref[g, pl.ds(2 * pair, 2)].reshape(2 * A_BLOCK, 2 * A_BLOCK)
        bias2_first = bias2 + no_prev
        for r in range(d):
            for qb in range(A_SPAN // d // A_BLOCK):
                q = cur_ref[0, r, qb * A_BLOCK:(qb + 1) * A_BLOCK, q_cols]
                if qb == 0:
                    keys = jnp.concatenate([prev_ref[0, r, :, k_cols], cur_ref[0, r, :A_BLOCK, k_cols]], axis=0)
                    vals = jnp.concatenate([prev_ref[0, r, :, v_cols], cur_ref[0, r, :A_BLOCK, v_cols]], axis=0)
                else:
                    keys = cur_ref[0, r, (qb - 1) * A_BLOCK:(qb + 1) * A_BLOCK, k_cols]
                    vals = cur_ref[0, r, (qb - 1) * A_BLOCK:(qb + 1) * A_BLOCK, v_cols]
                zero = jnp.zeros_like(q)
                q2 = jnp.concatenate([jnp.where(head0, q, zero), jnp.where(head0, zero, q)], axis=0)
                s = _dot_t(q2, keys) + (bias2_first if qb == 0 else bias2)
                m = jnp.max(s, axis=-1, keepdims=True)
                e = jnp.exp((s - m).astype(BF16))
                pvd = _dot(e, jnp.concatenate([vals, ones], axis=1))
                if d == 1:
                    rows = pl.ds(qb * A_BLOCK, A_BLOCK)
                elif _phase_major(d):
                    rows = pl.ds(r * _phase_pitch(d) + qb * A_BLOCK, A_BLOCK)
                else:
                    rows = pl.ds(qb * A_BLOCK * d + r, A_BLOCK, stride=d)
                pv_bufs[g][rows, :] = jnp.where(head0, pvd[:A_BLOCK, :LANES], pvd[A_BLOCK:, :LANES])
                den_bufs[g][rows, :] = jnp.where(head0, pvd[:A_BLOCK, LANES:], pvd[A_BLOCK:, LANES:])
                m_bufs[g][rows, :] = jnp.where(head0, m[:A_BLOCK], m[A_BLOCK:])

    def token_rows(buf, d, t0, n):
        if not _phase_major(d):
            return buf[pl.ds(t0, n), :]
        return jnp.concatenate([buf[pl.ds(step, d, stride=_phase_pitch(d)), :]
                                for step in range(t0 // d, (t0 + n) // d)], axis=0)

    chunk = 2 * A_BLOCK
    for c in range(A_SPAN // chunk):
        group_rows = lambda group_bufs: [token_rows(group_bufs[g], d, c * chunk, chunk)
                                         for g, d in enumerate(A_DILATIONS)]
        ms = group_rows(m_bufs)
        m = functools.reduce(jnp.maximum, ms)
        ws = [jnp.exp(mg - m) for mg in ms]
        num = sum(w * pv for w, pv in zip(ws, group_rows(pv_bufs)))
        den = sum(w * dn for w, dn in zip(ws, group_rows(den_bufs)))
        out_ref[0, pl.ds(c * chunk, chunk), :] = (num / den).astype(BF16)


def _attention(qkv, bias):
    B, _, S, _ = qkv[0].shape
    n_pairs = A_GROUP_WIDTH // LANES
    in_specs, args = [], []
    for g, d in enumerate(A_DILATIONS):
        steps = A_SPAN // d
        bps = steps // A_BLOCK
        in_specs += [
            pl.BlockSpec((1, d, steps, 3 * LANES), lambda b, i, pr: (b, 0, i, pr)),
            pl.BlockSpec((1, d, A_BLOCK, 3 * LANES),
                         lambda b, i, pr, bps=bps: (b, 0, jnp.maximum(i * bps - 1, 0), pr))]
        args += [qkv[g]] * 2
    buf_rows = [d * _phase_pitch(d) if _phase_major(d) else A_SPAN for d in A_DILATIONS]
    return pl.pallas_call(
        _attn_body,
        grid=(B, S // A_SPAN, n_pairs),
        in_specs=in_specs + [_resident(bias.shape)],
        out_specs=pl.BlockSpec((1, A_SPAN, LANES), lambda b, i, pr: (b, i, pr)),
        out_shape=jax.ShapeDtypeStruct((B, S, A_GROUP_WIDTH), BF16),
        scratch_shapes=[pltpu.VMEM((rows, LANES), F32) for rows in buf_rows] * 3,
        compiler_params=_params(3),
        name="attention",
    )(*args, bias)


def _t5_bucket(n):
    nf = np.maximum(n, 1).astype(np.float32)
    large = REL_MAX_EXACT + (np.log(nf / REL_MAX_EXACT) / np.log(REL_MAX_DIST / REL_MAX_EXACT)
                             * (REL_BUCKETS - REL_MAX_EXACT)).astype(np.int32)
    large = np.minimum(large, REL_BUCKETS - 1)
    return np.where(n < REL_MAX_EXACT, n, large).astype(np.int32)


def _band_buckets():
    i = np.arange(A_BLOCK)[:, None]
    j = np.arange(2 * A_BLOCK)[None, :]
    rel = A_BLOCK + i - j
    valid = (rel >= 0) & (rel <= A_BLOCK)
    return np.stack([np.where(valid, _t5_bucket(np.where(valid, rel, 0) * d), -1)
                     for d in A_DILATIONS]).astype(np.int32)


def _bias_body(table_ref, bucket_ref, o_ref):
    g = pl.program_id(0)
    bk = bucket_ref[0]
    for hd in range(A_HEADS):
        acc = jnp.full(bk.shape, NEG, F32)
        for b in range(REL_BUCKETS):
            acc = jnp.where(bk == b, table_ref[b, g * A_HEADS + hd], acc)
        o_ref[0, hd] = acc


def _band_bias(rel_table):
    buckets = jnp.asarray(_band_buckets())
    G, Q, Kk = buckets.shape
    return pl.pallas_call(
        _bias_body,
        grid=(G,),
        in_specs=[pl.BlockSpec(memory_space=pltpu.SMEM),
                  pl.BlockSpec((1, Q, Kk), lambda g: (g, 0, 0))],
        out_specs=pl.BlockSpec((1, A_HEADS, Q, Kk), lambda g: (g, 0, 0, 0)),
        out_shape=jax.ShapeDtypeStruct((G, A_HEADS, Q, Kk), F32),
        compiler_params=_params(1),
        name="band_bias",
    )(rel_table, buckets)


def _mixer_dilated(h, g_mix, layer, w_qkv, j, bias, *, tm, n_sub):
    return _attention(_qkv_proj(h, g_mix, layer, w_qkv, j, tm=tm, n_sub=n_sub), bias)


def _pool_body(h_ref, halo_ref, g_ref, win_ref, wgrp_ref, sc_ref, wout_ref, o_ref, *, tm, n_sub):
    i = pl.program_id(1)
    gain = g_ref[...]
    ts = tm // n_sub
    gw = h_ref.shape[2] // len(B_WINDOWS)
    xn = jnp.concatenate([_rms(halo_ref[0], gain), _rms(h_ref[0], gain)], axis=0).astype(BF16)
    ys = []
    for k in range(n_sub):
        y = _dot(xn[k * ts:(k + 1) * ts + B_HALO], win_ref[...])
        if k == 0:
            y = jnp.concatenate([jnp.where(i == 0, 0.0, y[:B_HALO]), y[B_HALO:]], axis=0)
        ys.append(y)
    pooled = []
    for k, y in enumerate(ys):
        pos = (lax.broadcasted_iota(jnp.int32, (ts, 1), 0) + (i * tm + k * ts + 1)).astype(F32)
        groups = []
        for g, win in enumerate(B_WINDOWS):
            yg = y[:, g * gw:(g + 1) * gw]
            s = yg
            sh = 1
            while sh < win:
                s = s + pltpu.roll(s, sh, axis=0)
                sh *= 2
            mean = s[B_HALO:] * (1.0 / jnp.minimum(pos, float(win)))
            groups.append((mean - yg[B_HALO:]).astype(BF16))
        pooled.append(groups)
    for k, groups in enumerate(pooled):
        z = jnp.concatenate([_dot(pg, wgrp_ref[g]) for g, pg in enumerate(groups)], axis=1) * sc_ref[...]
        rows = slice(k * ts, (k + 1) * ts)
        o_ref[0, rows, :] = h_ref[0, rows, :] + _dot(z.astype(BF16), wout_ref[...])


def _mixer_pool(h, g_mix, layer, w_in, w_grp, scale, w_out, j, *, tm, n_sub):
    B, S, D = h.shape
    hpt = tm // B_HALO
    return pl.pallas_call(
        functools.partial(_pool_body, tm=tm, n_sub=n_sub),
        grid=(B, S // tm),
        in_specs=[pl.BlockSpec((1, tm, D), lambda b, i: (b, i, 0)),
                  pl.BlockSpec((1, B_HALO, D), lambda b, i: (b, jnp.maximum(i * hpt - 1, 0), 0)),
                  _layer(g_mix, layer), _layer(w_in, j), _layer(w_grp, j), _layer(scale, j), _layer(w_out, j)],
        out_specs=pl.BlockSpec((1, tm, D), lambda b, i: (b, i, 0)),
        out_shape=jax.ShapeDtypeStruct((B, S, D), F32),
        compiler_params=_params(2),
        name="mixer_pool",
    )(h, h, g_mix, w_in, w_grp, scale, w_out)


def _sgu_body(h_ref, g_ref, win_ref, vg_ref, ws_ref, bs_ref, wout_ref, o_ref, *, tm, n_sub):
    D = h_ref.shape[2]
    gw = D // C_GROUPS
    t = lax.broadcasted_iota(jnp.int32, (C_CHUNK, C_CHUNK), 0)
    s = lax.broadcasted_iota(jnp.int32, (C_CHUNK, C_CHUNK), 1)
    wms = [jnp.where(s <= t, ws_ref[g], 0.0).astype(BF16) for g in range(C_GROUPS)]
    ts = tm // n_sub
    subs = [slice(k * ts, (k + 1) * ts) for k in range(n_sub)]
    xns = [_rms(h_ref[0, rows, :], g_ref[...]).astype(BF16) for rows in subs]
    zs = [(_dot(xn, win_ref[:, :D]), _dot(xn, win_ref[:, D:])) for xn in xns]
    uvs = []
    for zu, zv in zs:
        v = jax.nn.gelu(zv)
        mu = jnp.mean(v, axis=-1, keepdims=True)
        var = jnp.mean(jnp.square(v - mu), axis=-1, keepdims=True)
        vn = ((v - mu) * lax.rsqrt(var + EPS) * vg_ref[...]).astype(BF16)
        uvs.append((jax.nn.gelu(zu), vn))
    for (u, vn), rows in zip(uvs, subs):
        cols = []
        for g in range(C_GROUPS):
            b = bs_ref[:, g:g + 1]
            cols.append(jnp.concatenate(
                [_dot(wms[g], vn[n * C_CHUNK:(n + 1) * C_CHUNK, g * gw:(g + 1) * gw]) + b
                 for n in range(ts // C_CHUNK)], axis=0))
        sp = jnp.concatenate(cols, axis=1)
        o_ref[0, rows, :] = h_ref[0, rows, :] + _dot((u * sp).astype(BF16), wout_ref[...])


def _mixer_sgu(h, g_mix, layer, w_in, v_gain, w_s, b_s_t, w_out, j, *, tm, n_sub):
    B, S, D = h.shape
    return pl.pallas_call(
        functools.partial(_sgu_body, tm=tm, n_sub=n_sub),
        grid=(B, S // tm),
        in_specs=[pl.BlockSpec((1, tm, D), lambda b, i: (b, i, 0)),
                  _layer(g_mix, layer), _layer(w_in, j), _layer(v_gain, j),
                  _layer(w_s, j), _layer(b_s_t, j), _layer(w_out, j)],
        out_specs=pl.BlockSpec((1, tm, D), lambda b, i: (b, i, 0)),
        out_shape=jax.ShapeDtypeStruct((B, S, D), F32),
        compiler_params=_params(2),
        name="mixer_sgu",
    )(h, g_mix, w_in, v_gain, w_s, b_s_t, w_out)


def kernel(x, p, rel_table, norm_mix, norm_ffn, norm_ple, final_norm, a_w_qkv, a_w_o, b_w_in, b_w_grp, b_scale, b_w_out, c_w_in, c_v_gain, c_w_s, c_b_s, c_w_out, ffn_w_gate, ffn_w_up, ffn_w_down, ple_w_gate, ple_w_proj):
    depth = norm_mix.shape[0]
    rows = lambda v: v[..., None, :]
    bf = lambda w: w.astype(BF16)
    tm = 512
    g_mix, g_ffn, g_ple, g_last = rows(norm_mix), rows(norm_ffn), rows(norm_ple), rows(final_norm)
    a_w_qkv, a_w_o = bf(a_w_qkv), bf(a_w_o)
    b_w_in, b_w_grp, b_w_out = bf(b_w_in), bf(b_w_grp), bf(b_w_out)
    c_w_in, c_w_out = bf(c_w_in), bf(c_w_out)
    ffn_w_gate, ffn_w_up, ffn_w_down = bf(ffn_w_gate), bf(ffn_w_up), bf(ffn_w_down)
    ple_w_gate, ple_w_proj = bf(ple_w_gate), bf(ple_w_proj)
    bias = _band_bias(rel_table)
    h = x
    for i in range(depth):
        kind, j = i % 3, i // 3
        attn = None
        if kind == 0:
            attn = (_mixer_dilated(h, g_mix, i, a_w_qkv, j, bias, tm=1024, n_sub=2), a_w_o, j)
        elif kind == 1:
            h = _mixer_pool(h, g_mix, i, b_w_in, b_w_grp, rows(b_scale), b_w_out, j, tm=1024, n_sub=2)
        else:
            h = _mixer_sgu(h, g_mix, i, c_w_in, rows(c_v_gain), c_w_s, jnp.swapaxes(c_b_s, 1, 2), c_w_out, j,
                           tm=1024, n_sub=2)
        h = _ffn_ple(h, p, i, g_ffn, ffn_w_gate, ffn_w_up, ffn_w_down, g_ple, ple_w_gate, ple_w_proj,
                     g_last, attn, final_norm=(i == depth - 1), tm=1024, th=256, n_sub=2)
    return h
```

```python
import functools

import jax
import jax.numpy as jnp
import numpy as np
from jax import lax
from jax.experimental import pallas as pl
from jax.experimental.pallas import tpu as pltpu

EPS = 1e-6
NEG = -1e30

A_DILATIONS = (1, 4, 16)
A_GROUPS = 3
A_HEADS = 8
A_HEAD_DIM = 64
A_GROUP_WIDTH = A_HEADS * A_HEAD_DIM
A_BLOCK = 128
A_SPAN = A_BLOCK * max(A_DILATIONS)
REL_BUCKETS = 32
REL_MAX_EXACT = REL_BUCKETS // 2
REL_MAX_DIST = 2048
B_WINDOWS = (2, 4, 8, 16)
B_HALO = 16
C_CHUNK = 128
C_GROUPS = 4

LANES = 128
V7X_MXU_WIDTH = 256
V7X_VMEM_LIMIT_BYTES = 56 * 1024 * 1024

TILE_ROWS = 1024
SUB_TILES = 2
FFN_CHUNK = V7X_MXU_WIDTH

BF16 = jnp.bfloat16
F32 = jnp.float32


def _dot(a, b):
    return jnp.dot(a, b, preferred_element_type=F32)


def _dot_t(a, b):
    return lax.dot_general(a, b, (((1,), (1,)), ((), ())), preferred_element_type=F32)


def _rms(x, g):
    ms = jnp.mean(x * x, axis=-1, keepdims=True)
    return x * lax.rsqrt(ms + EPS) * g


def _resident(shape):
    nd = len(shape)
    return pl.BlockSpec(shape, lambda *_: (0,) * nd, pipeline_mode=pl.Buffered(1))


def _layer(arr, layer):
    nd = arr.ndim - 1
    return pl.BlockSpec((None,) + arr.shape[1:], lambda *_: (layer,) + (0,) * nd,
                        pipeline_mode=pl.Buffered(1))


def _params(n_grid_dims):
    return pltpu.CompilerParams(
        dimension_semantics=("parallel",) * n_grid_dims,
        vmem_limit_bytes=V7X_VMEM_LIMIT_BYTES,
    )


def _ffn_ple_body(h_ref, p_ref, gf_ref, wg_ref, wu_ref, wd_ref, gp_ref, wpg_ref, wpp_ref, gl_ref,
                  *refs, th, n_sub, final_norm, with_attn):
    o_ref = refs[-1]
    ts = h_ref.shape[1] // n_sub
    subs = [slice(k * ts, (k + 1) * ts) for k in range(n_sub)]
    hs = [h_ref[0, rows, :] for rows in subs]
    if with_attn:
        a_ref, wo_ref = refs[:2]
        hs = [h + _dot(a_ref[0, rows, :], wo_ref[...]) for h, rows in zip(hs, subs)]
    xns = [_rms(h, gf_ref[...]).astype(BF16) for h in hs]
    accs = [jnp.zeros_like(h) for h in hs]
    for c in range(wd_ref.shape[0] // th):
        cols = slice(c * th, (c + 1) * th)
        for k, xn in enumerate(xns):
            g = _dot(xn, wg_ref[:, cols])
            u = _dot(xn, wu_ref[:, cols])
            a = (g * jax.nn.sigmoid(g) * u).astype(BF16)
            accs[k] = accs[k] + _dot(a, wd_ref[cols, :])
    for h, acc, rows in zip(hs, accs, subs):
        h = h + acc
        gate = jax.nn.sigmoid(_dot(_rms(h, gp_ref[...]).astype(BF16), wpg_ref[...]))
        proj = _dot(p_ref[0, rows, :].astype(BF16), wpp_ref[...])
        h = h + proj * gate
        if final_norm:
            h = _rms(h, gl_ref[...])
        o_ref[0, rows, :] = h


def _ffn_ple(h, p, layer, g_ffn, w_gate, w_up, w_down, g_ple, w_pg, w_pp, g_last, attn=None, *,
             final_norm, tm, th, n_sub):
    B, S, D = h.shape
    body = functools.partial(_ffn_ple_body, th=th, n_sub=n_sub, final_norm=final_norm,
                             with_attn=attn is not None)
    tile = lambda w: pl.BlockSpec((1, tm, w), lambda b, i: (b, i, 0))
    in_specs = [tile(D), pl.BlockSpec((None, 1, tm, p.shape[-1]), lambda b, i: (layer, b, i, 0)),
                _layer(g_ffn, layer), _layer(w_gate, layer), _layer(w_up, layer), _layer(w_down, layer),
                _layer(g_ple, layer), _layer(w_pg, layer), _layer(w_pp, layer), _resident(g_last.shape)]
    args = [h, p, g_ffn, w_gate, w_up, w_down, g_ple, w_pg, w_pp, g_last]
    if attn is not None:
        o, w_o, j = attn
        in_specs += [tile(o.shape[-1]), _layer(w_o, j)]
        args += [o, w_o]
    return pl.pallas_call(
        body,
        grid=(B, S // tm),
        in_specs=in_specs,
        out_specs=tile(D),
        out_shape=jax.ShapeDtypeStruct((B, S, D), F32),
        compiler_params=_params(2),
        name="ffn_ple",
    )(*args)


def _qkv_body(h_ref, g_ref, w_ref, *refs, tm, n_sub):
    outs, xbuf = refs[:A_GROUPS], refs[A_GROUPS]
    W = A_GROUP_WIDTH
    n_slab = h_ref.shape[2] // LANES
    ts = tm // n_sub
    xns = []
    for k in range(n_sub):
        xn = _rms(h_ref[0, k * ts:(k + 1) * ts, :], g_ref[...])
        for s in range(n_slab):
            xbuf[k, s] = xn[:, s * LANES:(s + 1) * LANES]
        xns.append(xn.astype(BF16))
    for g, d in enumerate(A_DILATIONS):
        for k in range(n_sub):
            if d == 1:
                lhs = xns[k]
            else:
                lhs = jnp.concatenate(
                    [jnp.concatenate([xbuf[k, s, pl.ds(r, ts // d, stride=d), :] for s in range(n_slab)],
                                     axis=1) for r in range(d)], axis=0).astype(BF16)
            for part in range(3):
                c = part * A_GROUPS + g
                y = _dot(lhs, w_ref[:, c * W:(c + 1) * W])
                if part == 0:
                    y = y * (A_HEAD_DIM ** -0.5)
                y = y.astype(BF16)
                for r in range(d):
                    rows_in = slice(r * (ts // d), (r + 1) * (ts // d))
                    rows_out = slice(k * (ts // d), (k + 1) * (ts // d))
                    for pr in range(W // LANES):
                        lo = (3 * pr + part) * LANES
                        outs[g][0, r, rows_out, lo:lo + LANES] = y[rows_in, pr * LANES:(pr + 1) * LANES]


def _qkv_proj(h, g, layer, w, j, *, tm, n_sub):
    B, S, D = h.shape
    W3 = 3 * A_GROUP_WIDTH
    return pl.pallas_call(
        functools.partial(_qkv_body, tm=tm, n_sub=n_sub),
        grid=(B, S // tm),
        in_specs=[pl.BlockSpec((1, tm, D), lambda b, i: (b, i, 0)), _layer(g, layer), _layer(w, j)],
        out_specs=[pl.BlockSpec((1, d, tm // d, W3), lambda b, i: (b, 0, i, 0)) for d in A_DILATIONS],
        out_shape=[jax.ShapeDtypeStruct((B, d, S // d, W3), BF16) for d in A_DILATIONS],
        scratch_shapes=[pltpu.VMEM((n_sub, D // LANES, tm // n_sub, LANES), F32)],
        compiler_params=_params(2),
        name="qkv_proj",
    )(h, g, w)


def _phase_major(d):
    return d % 8 == 0


def _phase_pitch(d):
    return A_SPAN // d + 8


def _attn_body(*refs):
    n_in = 2 * A_GROUPS
    bias_ref, out_ref = refs[n_in], refs[n_in + 1]
    bufs = refs[n_in + 2:]
    pv_bufs, m_bufs, den_bufs = bufs[:A_GROUPS], bufs[A_GROUPS:2 * A_GROUPS], bufs[2 * A_GROUPS:]
    first = pl.program_id(1) == 0
    pair = pl.program_id(2)
    head0 = lax.broadcasted_iota(jnp.int32, (1, LANES), 1) < A_HEAD_DIM
    col = lax.broadcasted_iota(jnp.int32, (1, 2 * A_BLOCK), 1)
    no_prev = jnp.where(jnp.logical_and(first, col < A_BLOCK), NEG, 0.0).astype(F32)
    ones = jnp.ones((2 * A_BLOCK, LANES), BF16)
    q_cols, k_cols, v_cols = (slice(part * LANES, (part + 1) * LANES) for part in range(3))

    def token_rows(buf, d, t0, n):
        if not _phase_major(d):
            return buf[pl.ds(t0, n), :]
        return jnp.concatenate([buf[pl.ds(step, d, stride=_phase_pitch(d)), :]
                                for step in range(t0 // d, (t0 + n) // d)], axis=0)

    chunk = 2 * A_BLOCK

    def combine(c):
        group_rows = lambda group_bufs: [token_rows(group_bufs[g], d, c * chunk, chunk)
                                         for g, d in enumerate(A_DILATIONS)]
        ms = group_rows(m_bufs)
        m = functools.reduce(jnp.maximum, ms)
        ws = [jnp.exp(mg - m) for mg in ms]
        num = sum(w * pv for w, pv in zip(ws, group_rows(pv_bufs)))
        den = sum(w * dn for w, dn in zip(ws, group_rows(den_bufs)))
        out_ref[0, pl.ds(c * chunk, chunk), :] = (num / den).astype(BF16)

    for g, d in sorted(enumerate(A_DILATIONS), key=lambda gd: -gd[1]):
        cur_ref, prev_ref = refs[2 * g:2 * g + 2]
        bias2 = bias_ref[g, pl.ds(2 * pair, 2)].reshape(2 * A_BLOCK, 2 * A_BLOCK)
        bias2_first = bias2 + no_prev
        for r in range(d):
            for qb in range(A_SPAN // d // A_BLOCK):
                q = cur_ref[0, r, qb * A_BLOCK:(qb + 1) * A_BLOCK, q_cols]
                if qb == 0:
                    keys = jnp.concatenate([prev_ref[0, r, :, k_cols], cur_ref[0, r, :A_BLOCK, k_cols]], axis=0)
                    vals = jnp.concatenate([prev_ref[0, r, :, v_cols], cur_ref[0, r, :A_BLOCK, v_cols]], axis=0)
                else:
                    keys = cur_ref[0, r, (qb - 1) * A_BLOCK:(qb + 1) * A_BLOCK, k_cols]
                    vals = cur_ref[0, r, (qb - 1) * A_BLOCK:(qb + 1) * A_BLOCK, v_cols]
                zero = jnp.zeros_like(q)
                q2 = jnp.concatenate([jnp.where(head0, q, zero), jnp.where(head0, zero, q)], axis=0)
                s = _dot_t(q2, keys) + (bias2_first if qb == 0 else bias2)
                m = jnp.max(s, axis=-1, keepdims=True)
                e = jnp.exp((s - m).astype(BF16))
                pvd = _dot(e, jnp.concatenate([vals, ones], axis=1))
                if d == 1:
                    rows = pl.ds(qb * A_BLOCK, A_BLOCK)
                elif _phase_major(d):
                    rows = pl.ds(r * _phase_pitch(d) + qb * A_BLOCK, A_BLOCK)
                else:
                    rows = pl.ds(qb * A_BLOCK * d + r, A_BLOCK, stride=d)
                pv_bufs[g][rows, :] = jnp.where(head0, pvd[:A_BLOCK, :LANES], pvd[A_BLOCK:, :LANES])
                den_bufs[g][rows, :] = jnp.where(head0, pvd[:A_BLOCK, LANES:], pvd[A_BLOCK:, LANES:])
                m_bufs[g][rows, :] = jnp.where(head0, m[:A_BLOCK], m[A_BLOCK:])
                if d == 1 and (qb + 1) * A_BLOCK % chunk == 0:
                    combine((qb + 1) * A_BLOCK // chunk - 1)


def _attention(qkv, bias):
    B, _, S, _ = qkv[0].shape
    n_pairs = A_GROUP_WIDTH // LANES
    in_specs, args = [], []
    for g, d in enumerate(A_DILATIONS):
        steps = A_SPAN // d
        bps = steps // A_BLOCK
        in_specs += [
            pl.BlockSpec((1, d, steps, 3 * LANES), lambda b, i, pr: (b, 0, i, pr)),
            pl.BlockSpec((1, d, A_BLOCK, 3 * LANES),
                         lambda b, i, pr, bps=bps: (b, 0, jnp.maximum(i * bps - 1, 0), pr))]
        args += [qkv[g]] * 2
    buf_rows = [d * _phase_pitch(d) if _phase_major(d) else A_SPAN for d in A_DILATIONS]
    return pl.pallas_call(
        _attn_body,
        grid=(B, S // A_SPAN, n_pairs),
        in_specs=in_specs + [_resident(bias.shape)],
        out_specs=pl.BlockSpec((1, A_SPAN, LANES), lambda b, i, pr: (b, i, pr)),
        out_shape=jax.ShapeDtypeStruct((B, S, A_GROUP_WIDTH), BF16),
        scratch_shapes=[pltpu.VMEM((rows, LANES), F32) for rows in buf_rows] * 3,
        compiler_params=_params(3),
        name="attention",
    )(*args, bias)


def _t5_bucket(n):
    nf = np.maximum(n, 1).astype(np.float32)
    large = REL_MAX_EXACT + (np.log(nf / REL_MAX_EXACT) / np.log(REL_MAX_DIST / REL_MAX_EXACT)
                             * (REL_BUCKETS - REL_MAX_EXACT)).astype(np.int32)
    large = np.minimum(large, REL_BUCKETS - 1)
    return np.where(n < REL_MAX_EXACT, n, large).astype(np.int32)


def _band_buckets():
    i = np.arange(A_BLOCK)[:, None]
    j = np.arange(2 * A_BLOCK)[None, :]
    rel = A_BLOCK + i - j
    valid = (rel >= 0) & (rel <= A_BLOCK)
    return np.stack([np.where(valid, _t5_bucket(np.where(valid, rel, 0) * d), -1)
                     for d in A_DILATIONS]).astype(np.int32)


def _bias_body(table_ref, bucket_ref, o_ref):
    g = pl.program_id(0)
    bk = bucket_ref[0]
    for hd in range(A_HEADS):
        acc = jnp.full(bk.shape, NEG, F32)
        for b in range(REL_BUCKETS):
            acc = jnp.where(bk == b, table_ref[b, g * A_HEADS + hd], acc)
        o_ref[0, hd] = acc


def _band_bias(rel_table):
    buckets = jnp.asarray(_band_buckets())
    G, Q, Kk = buckets.shape
    return pl.pallas_call(
        _bias_body,
        grid=(G,),
        in_specs=[pl.BlockSpec(memory_space=pltpu.SMEM),
                  pl.BlockSpec((1, Q, Kk), lambda g: (g, 0, 0))],
        out_specs=pl.BlockSpec((1, A_HEADS, Q, Kk), lambda g: (g, 0, 0, 0)),
        out_shape=jax.ShapeDtypeStruct((G, A_HEADS, Q, Kk), F32),
        compiler_params=_params(1),
        name="band_bias",
    )(rel_table, buckets)


def _mixer_dilated(h, g_mix, layer, w_qkv, j, bias, *, tm, n_sub):
    return _attention(_qkv_proj(h, g_mix, layer, w_qkv, j, tm=tm, n_sub=n_sub), bias)


def _pool_body(h_ref, halo_ref, g_ref, win_ref, wgrp_ref, sc_ref, wout_ref, o_ref, *, tm, n_sub):
    i = pl.program_id(1)
    gain = g_ref[...]
    ts = tm // n_sub
    gw = h_ref.shape[2] // len(B_WINDOWS)
    xn = jnp.concatenate([_rms(halo_ref[0], gain), _rms(h_ref[0], gain)], axis=0).astype(BF16)
    ys = []
    for k in range(n_sub):
        y = _dot(xn[k * ts:(k + 1) * ts + B_HALO], win_ref[...])
        if k == 0:
            y = jnp.concatenate([jnp.where(i == 0, 0.0, y[:B_HALO]), y[B_HALO:]], axis=0)
        ys.append(y)
    pooled = []
    for k, y in enumerate(ys):
        pos = (lax.broadcasted_iota(jnp.int32, (ts, 1), 0) + (i * tm + k * ts + 1)).astype(F32)
        groups = []
        for g, win in enumerate(B_WINDOWS):
            yg = y[:, g * gw:(g + 1) * gw]
            s = yg
            sh = 1
            while sh < win:
                s = s + pltpu.roll(s, sh, axis=0)
                sh *= 2
            mean = s[B_HALO:] * (1.0 / jnp.minimum(pos, float(win)))
            groups.append((mean - yg[B_HALO:]).astype(BF16))
        pooled.append(groups)
    for k, groups in enumerate(pooled):
        z = jnp.concatenate([_dot(pg, wgrp_ref[g]) for g, pg in enumerate(groups)], axis=1) * sc_ref[...]
        rows = slice(k * ts, (k + 1) * ts)
        o_ref[0, rows, :] = h_ref[0, rows, :] + _dot(z.astype(BF16), wout_ref[...])


def _mixer_pool(h, g_mix, layer, w_in, w_grp, scale, w_out, j, *, tm, n_sub):
    B, S, D = h.shape
    hpt = tm // B_HALO
    return pl.pallas_call(
        functools.partial(_pool_body, tm=tm, n_sub=n_sub),
        grid=(B, S // tm),
        in_specs=[pl.BlockSpec((1, tm, D), lambda b, i: (b, i, 0)),
                  pl.BlockSpec((1, B_HALO, D), lambda b, i: (b, jnp.maximum(i * hpt - 1, 0), 0)),
                  _layer(g_mix, layer), _layer(w_in, j), _layer(w_grp, j), _layer(scale, j), _layer(w_out, j)],
        out_specs=pl.BlockSpec((1, tm, D), lambda b, i: (b, i, 0)),
        out_shape=jax.ShapeDtypeStruct((B, S, D), F32),
        compiler_params=_params(2),
        name="mixer_pool",
    )(h, h, g_mix, w_in, w_grp, scale, w_out)


def _sgu_body(h_ref, g_ref, win_ref, vg_ref, ws_ref, bs_ref, wout_ref, o_ref, *, tm, n_sub):
    D = h_ref.shape[2]
    gw = D // C_GROUPS
    t = lax.broadcasted_iota(jnp.int32, (C_CHUNK, C_CHUNK), 0)
    s = lax.broadcasted_iota(jnp.int32, (C_CHUNK, C_CHUNK), 1)
    wms = [jnp.where(s <= t, ws_ref[g], 0.0).astype(BF16) for g in range(C_GROUPS)]
    ts = tm // n_sub
    subs = [slice(k * ts, (k + 1) * ts) for k in range(n_sub)]
    xns = [_rms(h_ref[0, rows, :], g_ref[...]).astype(BF16) for rows in subs]
    zs = [(_dot(xn, win_ref[:, D:]), _dot(xn, win_ref[:, :D])) for xn in xns]
    uvs = []
    for zv, zu in zs:
        v = jax.nn.gelu(zv)
        mu = jnp.mean(v, axis=-1, keepdims=True)
        var = jnp.mean(jnp.square(v - mu), axis=-1, keepdims=True)
        vn = ((v - mu) * lax.rsqrt(var + EPS) * vg_ref[...]).astype(BF16)
        uvs.append((jax.nn.gelu(zu), vn))
    for (u, vn), rows in zip(uvs, subs):
        cols = []
        for g in range(C_GROUPS):
            b = bs_ref[:, g:g + 1]
            cols.append(jnp.concatenate(
                [_dot(wms[g], vn[n * C_CHUNK:(n + 1) * C_CHUNK, g * gw:(g + 1) * gw]) + b
                 for n in range(ts // C_CHUNK)], axis=0))
        sp = jnp.concatenate(cols, axis=1)
        o_ref[0, rows, :] = h_ref[0, rows, :] + _dot((u * sp).astype(BF16), wout_ref[...])


def _mixer_sgu(h, g_mix, layer, w_in, v_gain, w_s, b_s_t, w_out, j, *, tm, n_sub):
    B, S, D = h.shape
    return pl.pallas_call(
        functools.partial(_sgu_body, tm=tm, n_sub=n_sub),
        grid=(B, S // tm),
        in_specs=[pl.BlockSpec((1, tm, D), lambda b, i: (b, i, 0)),
                  _layer(g_mix, layer), _layer(w_in, j), _layer(v_gain, j),
                  _layer(w_s, j), _layer(b_s_t, j), _layer(w_out, j)],
        out_specs=pl.BlockSpec((1, tm, D), lambda b, i: (b, i, 0)),
        out_shape=jax.ShapeDtypeStruct((B, S, D), F32),
        compiler_params=_params(2),
        name="mixer_sgu",
    )(h, g_mix, w_in, v_gain, w_s, b_s_t, w_out)


def kernel(x, p, rel_table, norm_mix, norm_ffn, norm_ple, final_norm, a_w_qkv, a_w_o, b_w_in, b_w_grp, b_scale, b_w_out, c_w_in, c_v_gain, c_w_s, c_b_s, c_w_out, ffn_w_gate, ffn_w_up, ffn_w_down, ple_w_gate, ple_w_proj):
    depth = norm_mix.shape[0]
    rows = lambda v: v[..., None, :]
    bf = lambda w: w.astype(BF16)
    tiles = dict(tm=TILE_ROWS, n_sub=SUB_TILES)
    g_mix, g_ffn, g_ple, g_last = rows(norm_mix), rows(norm_ffn), rows(norm_ple), rows(final_norm)
    a_w_qkv, a_w_o = bf(a_w_qkv), bf(a_w_o)
    b_w_in, b_w_grp, b_w_out = bf(b_w_in), bf(b_w_grp), bf(b_w_out)
    c_w_in, c_w_out = bf(c_w_in), bf(c_w_out)
    ffn_w_gate, ffn_w_up, ffn_w_down = bf(ffn_w_gate), bf(ffn_w_up), bf(ffn_w_down)
    ple_w_gate, ple_w_proj = bf(ple_w_gate), bf(ple_w_proj)
    bias = _band_bias(rel_table)
    h = x
    for i in range(depth):
        kind, j = i % 3, i // 3
        attn = None
        if kind == 0:
            attn = (_mixer_dilated(h, g_mix, i, a_w_qkv, j, bias, **tiles), a_w_o, j)
        elif kind == 1:
            h = _mixer_pool(h, g_mix, i, b_w_in, b_w_grp, rows(b_scale), b_w_out, j, **tiles)
        else:
            h = _mixer_sgu(h, g_mix, i, c_w_in, rows(c_v_gain), c_w_s, jnp.swapaxes(c_b_s, 1, 2), c_w_out, j,
                           **tiles)
        h = _ffn_ple(h, p, i, g_ffn, ffn_w_gate, ffn_w_up, ffn_w_down, g_ple, ple_w_gate, ple_w_proj,
                     g_last, attn, final_norm=(i == depth - 1), th=FFN_CHUNK, **tiles)
    return h
```

```python
import functools

import jax
import jax.numpy as jnp
import numpy as np
from jax import lax
from jax.experimental import pallas as pl
from jax.experimental.pallas import tpu as pltpu

EPS = 1e-6
NEG = -1e30

A_DILATIONS = (1, 4, 16)
A_GROUPS = 3
A_HEADS = 8
A_HEAD_DIM = 64
A_GROUP_WIDTH = A_HEADS * A_HEAD_DIM
A_BLOCK = 128
A_SPAN = A_BLOCK * max(A_DILATIONS)
REL_BUCKETS = 32
REL_MAX_EXACT = REL_BUCKETS // 2
REL_MAX_DIST = 2048
B_WINDOWS = (2, 4, 8, 16)
B_HALO = 16
C_CHUNK = 128
C_GROUPS = 4

LANES = 128
V7X_MXU_WIDTH = 256
V7X_VMEM_LIMIT_BYTES = 56 * 1024 * 1024

TILE_ROWS = 1024
SUB_TILES = 2
FFN_CHUNK = V7X_MXU_WIDTH

BF16 = jnp.bfloat16
F32 = jnp.float32


def _dot(a, b):
    return jnp.dot(a, b, preferred_element_type=F32)


def _dot_t(a, b):
    return lax.dot_general(a, b, (((1,), (1,)), ((), ())), preferred_element_type=F32)


def _rms(x, g):
    ms = jnp.mean(x * x, axis=-1, keepdims=True)
    return x * lax.rsqrt(ms + EPS) * g


def _resident(shape):
    nd = len(shape)
    return pl.BlockSpec(shape, lambda *_: (0,) * nd, pipeline_mode=pl.Buffered(1))


def _layer(arr, layer):
    nd = arr.ndim - 1
    return pl.BlockSpec((None,) + arr.shape[1:], lambda *_: (layer,) + (0,) * nd,
                        pipeline_mode=pl.Buffered(1))


def _params(n_grid_dims):
    return pltpu.CompilerParams(
        dimension_semantics=("parallel",) * n_grid_dims,
        vmem_limit_bytes=V7X_VMEM_LIMIT_BYTES,
    )


def _ffn_ple_body(h_ref, p_ref, gf_ref, wg_ref, wu_ref, wd_ref, gp_ref, wpg_ref, wpp_ref, gl_ref,
                  *refs, th, n_sub, final_norm, with_attn):
    o_ref = refs[-1]
    ts = h_ref.shape[1] // n_sub
    subs = [slice(k * ts, (k + 1) * ts) for k in range(n_sub)]
    hs = [h_ref[0, rows, :] for rows in subs]
    if with_attn:
        a_ref, wo_ref = refs[:2]
        hs = [h + _dot(a_ref[0, rows, :], wo_ref[...]) for h, rows in zip(hs, subs)]
    xns = [_rms(h, gf_ref[...]).astype(BF16) for h in hs]
    acts = [[] for _ in hs]
    for c in range(wd_ref.shape[0] // th):
        cols = slice(c * th, (c + 1) * th)
        for k, xn in enumerate(xns):
            g = _dot(xn, wg_ref[:, cols])
            u = _dot(xn, wu_ref[:, cols])
            acts[k].append((g * jax.nn.sigmoid(g) * u).astype(BF16))
    accs = [_dot(jnp.concatenate(a, axis=1), wd_ref[...]) for a in acts]
    for h, acc, rows in zip(hs, accs, subs):
        h = h + acc
        gate = jax.nn.sigmoid(_dot(_rms(h, gp_ref[...]).astype(BF16), wpg_ref[...]))
        proj = _dot(p_ref[0, rows, :].astype(BF16), wpp_ref[...])
        h = h + proj * gate
        if final_norm:
            h = _rms(h, gl_ref[...])
        o_ref[0, rows, :] = h


def _ffn_ple(h, p, layer, g_ffn, w_gate, w_up, w_down, g_ple, w_pg, w_pp, g_last, attn=None, *,
             final_norm, tm, th, n_sub):
    B, S, D = h.shape
    body = functools.partial(_ffn_ple_body, th=th, n_sub=n_sub, final_norm=final_norm,
                             with_attn=attn is not None)
    tile = lambda w: pl.BlockSpec((1, tm, w), lambda b, i: (b, i, 0))
    in_specs = [tile(D), pl.BlockSpec((None, 1, tm, p.shape[-1]), lambda b, i: (layer, b, i, 0)),
                _layer(g_ffn, layer), _layer(w_gate, layer), _layer(w_up, layer), _layer(w_down, layer),
                _layer(g_ple, layer), _layer(w_pg, layer), _layer(w_pp, layer), _resident(g_last.shape)]
    args = [h, p, g_ffn, w_gate, w_up, w_down, g_ple, w_pg, w_pp, g_last]
    if attn is not None:
        o, w_o, j = attn
        in_specs += [tile(o.shape[-1]), _layer(w_o, j)]
        args += [o, w_o]
    return pl.pallas_call(
        body,
        grid=(B, S // tm),
        in_specs=in_specs,
        out_specs=tile(D),
        out_shape=jax.ShapeDtypeStruct((B, S, D), F32),
        compiler_params=_params(2),
        name="ffn_ple",
    )(*args)


def _qkv_body(h_ref, g_ref, w_ref, *refs, tm, n_sub):
    outs, xbuf = refs[:A_GROUPS], refs[A_GROUPS]
    W = A_GROUP_WIDTH
    n_slab = h_ref.shape[2] // LANES
    ts = tm // n_sub
    xns = []
    for k in range(n_sub):
        xn = _rms(h_ref[0, k * ts:(k + 1) * ts, :], g_ref[...])
        for s in range(n_slab):
            xbuf[k, s] = xn[:, s * LANES:(s + 1) * LANES]
        xns.append(xn.astype(BF16))
    for g, d in enumerate(A_DILATIONS):
        for k in range(n_sub):
            if d == 1:
                lhs = xns[k]
            else:
                lhs = jnp.concatenate(
                    [jnp.concatenate([xbuf[k, s, pl.ds(r, ts // d, stride=d), :] for s in range(n_slab)],
                                     axis=1) for r in range(d)], axis=0).astype(BF16)
            for part in range(3):
                c = part * A_GROUPS + g
                y = _dot(lhs, w_ref[:, c * W:(c + 1) * W])
                if part == 0:
                    y = y * (A_HEAD_DIM ** -0.5)
                y = y.astype(BF16)
                for r in range(d):
                    rows_in = slice(r * (ts // d), (r + 1) * (ts // d))
                    rows_out = slice(k * (ts // d), (k + 1) * (ts // d))
                    for pr in range(W // LANES):
                        lo = (3 * pr + part) * LANES
                        outs[g][0, r, rows_out, lo:lo + LANES] = y[rows_in, pr * LANES:(pr + 1) * LANES]


def _qkv_proj(h, g, layer, w, j, *, tm, n_sub):
    B, S, D = h.shape
    W3 = 3 * A_GROUP_WIDTH
    return pl.pallas_call(
        functools.partial(_qkv_body, tm=tm, n_sub=n_sub),
        grid=(B, S // tm),
        in_specs=[pl.BlockSpec((1, tm, D), lambda b, i: (b, i, 0)), _layer(g, layer), _layer(w, j)],
        out_specs=[pl.BlockSpec((1, d, tm // d, W3), lambda b, i: (b, 0, i, 0)) for d in A_DILATIONS],
        out_shape=[jax.ShapeDtypeStruct((B, d, S // d, W3), BF16) for d in A_DILATIONS],
        scratch_shapes=[pltpu.VMEM((n_sub, D // LANES, tm // n_sub, LANES), F32)],
        compiler_params=_params(2),
        name="qkv_proj",
    )(h, g, w)


def _phase_major(d):
    return d % 8 == 0


def _phase_pitch(d):
    return A_SPAN // d + 8


def _attn_body(*refs):
    n_in = 2 * A_GROUPS
    bias_ref, out_ref = refs[n_in], refs[n_in + 1]
    bufs = refs[n_in + 2:]
    pv_bufs, m_bufs, den_bufs = bufs[:A_GROUPS], bufs[A_GROUPS:2 * A_GROUPS], bufs[2 * A_GROUPS:]
    first = pl.program_id(1) == 0
    pair = pl.program_id(2)
    head0 = lax.broadcasted_iota(jnp.int32, (1, LANES), 1) < A_HEAD_DIM
    col = lax.broadcasted_iota(jnp.int32, (1, 2 * A_BLOCK), 1)
    no_prev = jnp.where(jnp.logical_and(first, col < A_BLOCK), NEG, 0.0).astype(F32)
    ones = jnp.ones((2 * A_BLOCK, LANES), BF16)
    q_cols, k_cols, v_cols = (slice(part * LANES, (part + 1) * LANES) for part in range(3))

    def token_rows(buf, d, t0, n):
        if not _phase_major(d):
            return buf[pl.ds(t0, n), :]
        return jnp.concatenate([buf[pl.ds(step, d, stride=_phase_pitch(d)), :]
                                for step in range(t0 // d, (t0 + n) // d)], axis=0)

    chunk = 2 * A_BLOCK

    def combine(c):
        group_rows = lambda group_bufs: [token_rows(group_bufs[g], d, c * chunk, chunk)
                                         for g, d in enumerate(A_DILATIONS)]
        ms = group_rows(m_bufs)
        m = functools.reduce(jnp.maximum, ms)
        ws = [jnp.exp(mg - m) for mg in ms]
        num = sum(w * pv for w, pv in zip(ws, group_rows(pv_bufs)))
        den = sum(w * dn for w, dn in zip(ws, group_rows(den_bufs)))
        out_ref[0, pl.ds(c * chunk, chunk), :] = (num / den).astype(BF16)

    for g, d in sorted(enumerate(A_DILATIONS), key=lambda gd: -gd[1]):
        cur_ref, prev_ref = refs[2 * g:2 * g + 2]
        bias2 = bias_ref[g, pl.ds(2 * pair, 2)].reshape(2 * A_BLOCK, 2 * A_BLOCK)
        bias2_first = bias2 + no_prev
        for r in range(d):
            for qb in range(A_SPAN // d // A_BLOCK):
                q = cur_ref[0, r, qb * A_BLOCK:(qb + 1) * A_BLOCK, q_cols]
                if qb == 0:
                    keys = jnp.concatenate([prev_ref[0, r, :, k_cols], cur_ref[0, r, :A_BLOCK, k_cols]], axis=0)
                    vals = jnp.concatenate([prev_ref[0, r, :, v_cols], cur_ref[0, r, :A_BLOCK, v_cols]], axis=0)
                else:
                    keys = cur_ref[0, r, (qb - 1) * A_BLOCK:(qb + 1) * A_BLOCK, k_cols]
                    vals = cur_ref[0, r, (qb - 1) * A_BLOCK:(qb + 1) * A_BLOCK, v_cols]
                zero = jnp.zeros_like(q)
                q2 = jnp.concatenate([jnp.where(head0, q, zero), jnp.where(head0, zero, q)], axis=0)
                s = _dot_t(q2, keys) + (bias2_first if qb == 0 else bias2)
                m = jnp.max(s, axis=-1, keepdims=True)
                e = jnp.exp((s - m).astype(BF16))
                pvd = _dot(e, jnp.concatenate([vals, ones], axis=1))
                if d == 1:
                    rows = pl.ds(qb * A_BLOCK, A_BLOCK)
                elif _phase_major(d):
                    rows = pl.ds(r * _phase_pitch(d) + qb * A_BLOCK, A_BLOCK)
                else:
                    rows = pl.ds(qb * A_BLOCK * d + r, A_BLOCK, stride=d)
                pv_bufs[g][rows, :] = jnp.where(head0, pvd[:A_BLOCK, :LANES], pvd[A_BLOCK:, :LANES])
                den_bufs[g][rows, :] = jnp.where(head0, pvd[:A_BLOCK, LANES:], pvd[A_BLOCK:, LANES:])
                m_bufs[g][rows, :] = jnp.where(head0, m[:A_BLOCK], m[A_BLOCK:])
                if d == 1 and (qb + 1) * A_BLOCK % chunk == 0:
                    combine((qb + 1) * A_BLOCK // chunk - 1)


def _attention(qkv, bias):
    B, _, S, _ = qkv[0].shape
    n_pairs = A_GROUP_WIDTH // LANES
    in_specs, args = [], []
    for g, d in enumerate(A_DILATIONS):
        steps = A_SPAN // d
        bps = steps // A_BLOCK
        in_specs += [
            pl.BlockSpec((1, d, steps, 3 * LANES), lambda b, i, pr: (b, 0, i, pr)),
            pl.BlockSpec((1, d, A_BLOCK, 3 * LANES),
                         lambda b, i, pr, bps=bps: (b, 0, jnp.maximum(i * bps - 1, 0), pr))]
        args += [qkv[g]] * 2
    buf_rows = [d * _phase_pitch(d) if _phase_major(d) else A_SPAN for d in A_DILATIONS]
    return pl.pallas_call(
        _attn_body,
        grid=(B, S // A_SPAN, n_pairs),
        in_specs=in_specs + [_resident(bias.shape)],
        out_specs=pl.BlockSpec((1, A_SPAN, LANES), lambda b, i, pr: (b, i, pr)),
        out_shape=jax.ShapeDtypeStruct((B, S, A_GROUP_WIDTH), BF16),
        scratch_shapes=[pltpu.VMEM((rows, LANES), F32) for rows in buf_rows] * 3,
        compiler_params=_params(3),
        name="attention",
    )(*args, bias)


def _t5_bucket(n):
    nf = np.maximum(n, 1).astype(np.float32)
    large = REL_MAX_EXACT + (np.log(nf / REL_MAX_EXACT) / np.log(REL_MAX_DIST / REL_MAX_EXACT)
                             * (REL_BUCKETS - REL_MAX_EXACT)).astype(np.int32)
    large = np.minimum(large, REL_BUCKETS - 1)
    return np.where(n < REL_MAX_EXACT, n, large).astype(np.int32)


def _band_buckets():
    i = np.arange(A_BLOCK)[:, None]
    j = np.arange(2 * A_BLOCK)[None, :]
    rel = A_BLOCK + i - j
    valid = (rel >= 0) & (rel <= A_BLOCK)
    return np.stack([np.where(valid, _t5_bucket(np.where(valid, rel, 0) * d), -1)
                     for d in A_DILATIONS]).astype(np.int32)


def _bias_body(table_ref, bucket_ref, o_ref):
    g = pl.program_id(0)
    bk = bucket_ref[0]
    for hd in range(A_HEADS):
        acc = jnp.full(bk.shape, NEG, F32)
        for b in range(REL_BUCKETS):
            acc = jnp.where(bk == b, table_ref[b, g * A_HEADS + hd], acc)
        o_ref[0, hd] = acc


def _band_bias(rel_table):
    buckets = jnp.asarray(_band_buckets())
    G, Q, Kk = buckets.shape
    return pl.pallas_call(
        _bias_body,
        grid=(G,),
        in_specs=[pl.BlockSpec(memory_space=pltpu.SMEM),
                  pl.BlockSpec((1, Q, Kk), lambda g: (g, 0, 0))],
        out_specs=pl.BlockSpec((1, A_HEADS, Q, Kk), lambda g: (g, 0, 0, 0)),
        out_shape=jax.ShapeDtypeStruct((G, A_HEADS, Q, Kk), F32),
        compiler_params=_params(1),
        name="band_bias",
    )(rel_table, buckets)


def _mixer_dilated(h, g_mix, layer, w_qkv, j, bias, *, tm, n_sub):
    return _attention(_qkv_proj(h, g_mix, layer, w_qkv, j, tm=tm, n_sub=n_sub), bias)


def _pool_body(h_ref, halo_ref, g_ref, win_ref, wgrp_ref, sc_ref, wout_ref, o_ref, *, tm, n_sub):
    i = pl.program_id(1)
    gain = g_ref[...]
    ts = tm // n_sub
    gw = h_ref.shape[2] // len(B_WINDOWS)
    xn = jnp.concatenate([_rms(halo_ref[0], gain), _rms(h_ref[0], gain)], axis=0).astype(BF16)
    ys = []
    for k in range(n_sub):
        y = _dot(xn[k * ts:(k + 1) * ts + B_HALO], win_ref[...])
        if k == 0:
            y = jnp.concatenate([jnp.where(i == 0, 0.0, y[:B_HALO]), y[B_HALO:]], axis=0)
        ys.append(y)
    pooled = []
    for k, y in enumerate(ys):
        pos = (lax.broadcasted_iota(jnp.int32, (ts, 1), 0) + (i * tm + k * ts + 1)).astype(F32)
        groups = []
        for g, win in enumerate(B_WINDOWS):
            yg = y[:, g * gw:(g + 1) * gw]
            s = yg
            sh = 1
            while sh < win:
                s = s + pltpu.roll(s, sh, axis=0)
                sh *= 2
            mean = s[B_HALO:] * (1.0 / jnp.minimum(pos, float(win)))
            groups.append((mean - yg[B_HALO:]).astype(BF16))
        pooled.append(groups)
    for k, groups in enumerate(pooled):
        z = jnp.concatenate([_dot(pg, wgrp_ref[g]) for g, pg in enumerate(groups)], axis=1) * sc_ref[...]
        rows = slice(k * ts, (k + 1) * ts)
        o_ref[0, rows, :] = h_ref[0, rows, :] + _dot(z.astype(BF16), wout_ref[...])


def _mixer_pool(h, g_mix, layer, w_in, w_grp, scale, w_out, j, *, tm, n_sub):
    B, S, D = h.shape
    hpt = tm // B_HALO
    return pl.pallas_call(
        functools.partial(_pool_body, tm=tm, n_sub=n_sub),
        grid=(B, S // tm),
        in_specs=[pl.BlockSpec((1, tm, D), lambda b, i: (b, i, 0)),
                  pl.BlockSpec((1, B_HALO, D), lambda b, i: (b, jnp.maximum(i * hpt - 1, 0), 0)),
                  _layer(g_mix, layer), _layer(w_in, j), _layer(w_grp, j), _layer(scale, j), _layer(w_out, j)],
        out_specs=pl.BlockSpec((1, tm, D), lambda b, i: (b, i, 0)),
        out_shape=jax.ShapeDtypeStruct((B, S, D), F32),
        compiler_params=_params(2),
        name="mixer_pool",
    )(h, h, g_mix, w_in, w_grp, scale, w_out)


def _sgu_body(h_ref, g_ref, win_ref, vg_ref, ws_ref, bs_ref, wout_ref, o_ref, *, tm, n_sub):
    D = h_ref.shape[2]
    gw = D // C_GROUPS
    t = lax.broadcasted_iota(jnp.int32, (C_CHUNK, C_CHUNK), 0)
    s = lax.broadcasted_iota(jnp.int32, (C_CHUNK, C_CHUNK), 1)
    wms = [jnp.where(s <= t, ws_ref[g], 0.0).astype(BF16) for g in range(C_GROUPS)]
    ts = tm // n_sub
    subs = [slice(k * ts, (k + 1) * ts) for k in range(n_sub)]
    xns = [_rms(h_ref[0, rows, :], g_ref[...]).astype(BF16) for rows in subs]
    zs = [(_dot(xn, win_ref[:, D:]), _dot(xn, win_ref[:, :D])) for xn in xns]
    uvs = []
    for zv, zu in zs:
        v = jax.nn.gelu(zv)
        mu = jnp.mean(v, axis=-1, keepdims=True)
        var = jnp.mean(jnp.square(v - mu), axis=-1, keepdims=True)
        vn = ((v - mu) * lax.rsqrt(var + EPS) * vg_ref[...]).astype(BF16)
        uvs.append((jax.nn.gelu(zu), vn))
    for (u, vn), rows in zip(uvs, subs):
        cols = []
        for g in range(C_GROUPS):
            b = bs_ref[:, g:g + 1]
            cols.append(jnp.concatenate(
                [_dot(wms[g], vn[n * C_CHUNK:(n + 1) * C_CHUNK, g * gw:(g + 1) * gw]) + b
                 for n in range(ts // C_CHUNK)], axis=0))
        sp = jnp.concatenate(cols, axis=1)
        o_ref[0, rows, :] = h_ref[0, rows, :] + _dot((u * sp).astype(BF16), wout_ref[...])


def _mixer_sgu(h, g_mix, layer, w_in, v_gain, w_s, b_s_t, w_out, j, *, tm, n_sub):
    B, S, D = h.shape
    return pl.pallas_call(
        functools.partial(_sgu_body, tm=tm, n_sub=n_sub),
        grid=(B, S // tm),
        in_specs=[pl.BlockSpec((1, tm, D), lambda b, i: (b, i, 0)),
                  _layer(g_mix, layer), _layer(w_in, j), _layer(v_gain, j),
                  _layer(w_s, j), _layer(b_s_t, j), _layer(w_out, j)],
        out_specs=pl.BlockSpec((1, tm, D), lambda b, i: (b, i, 0)),
        out_shape=jax.ShapeDtypeStruct((B, S, D), F32),
        compiler_params=_params(2),
        name="mixer_sgu",
    )(h, g_mix, w_in, v_gain, w_s, b_s_t, w_out)


def kernel(x, p, rel_table, norm_mix, norm_ffn, norm_ple, final_norm, a_w_qkv, a_w_o, b_w_in, b_w_grp, b_scale, b_w_out, c_w_in, c_v_gain, c_w_s, c_b_s, c_w_out, ffn_w_gate, ffn_w_up, ffn_w_down, ple_w_gate, ple_w_proj):
    depth = norm_mix.shape[0]
    rows = lambda v: v[..., None, :]
    bf = lambda w: w.astype(BF16)
    tiles = dict(tm=TILE_ROWS, n_sub=SUB_TILES)
    g_mix, g_ffn, g_ple, g_last = rows(norm_mix), rows(norm_ffn), rows(norm_ple), rows(final_norm)
    a_w_qkv, a_w_o = bf(a_w_qkv), bf(a_w_o)
    b_w_in, b_w_grp, b_w_out = bf(b_w_in), bf(b_w_grp), bf(b_w_out)
    c_w_in, c_w_out = bf(c_w_in), bf(c_w_out)
    ffn_w_gate, ffn_w_up, ffn_w_down = bf(ffn_w_gate), bf(ffn_w_up), bf(ffn_w_down)
    ple_w_gate, ple_w_proj = bf(ple_w_gate), bf(ple_w_proj)
    bias = _band_bias(rel_table)
    h = x
    for i in range(depth):
        kind, j = i % 3, i // 3
        attn = None
        if kind == 0:
            attn = (_mixer_dilated(h, g_mix, i, a_w_qkv, j, bias, **tiles), a_w_o, j)
        elif kind == 1:
            h = _mixer_pool(h, g_mix, i, b_w_in, b_w_grp, rows(b_scale), b_w_out, j, **tiles)
        else:
            h = _mixer_sgu(h, g_mix, i, c_w_in, rows(c_v_gain), c_w_s, jnp.swapaxes(c_b_s, 1, 2), c_w_out, j,
                           **tiles)
        h = _ffn_ple(h, p, i, g_ffn, ffn_w_gate, ffn_w_up, ffn_w_down, g_ple, ple_w_gate, ple_w_proj,
                     g_last, attn, final_norm=(i == depth - 1), th=FFN_CHUNK, **tiles)
    return h
```

```python
import functools

import jax
import jax.numpy as jnp
import numpy as np
from jax import lax
from jax.experimental import pallas as pl
from jax.experimental.pallas import tpu as pltpu

EPS = 1e-6
NEG = -1e30

A_DILATIONS = (1, 4, 16)
A_GROUPS = 3
A_HEADS = 8
A_HEAD_DIM = 64
A_GROUP_WIDTH = A_HEADS * A_HEAD_DIM
A_BLOCK = 128
A_SPAN = A_BLOCK * max(A_DILATIONS)
REL_BUCKETS = 32
REL_MAX_EXACT = REL_BUCKETS // 2
REL_MAX_DIST = 2048
B_WINDOWS = (2, 4, 8, 16)
B_HALO = 16
C_CHUNK = 128
C_GROUPS = 4

LANES = 128
BF16_SUBLANES = 16
V7X_MXU_WIDTH = 256
V7X_VMEM_LIMIT_BYTES = 56 * 1024 * 1024

TILE_ROWS = 1024
SUB_TILES = 2
FFN_CHUNK = V7X_MXU_WIDTH

BF16 = jnp.bfloat16
F32 = jnp.float32


def _dot(a, b):
    return jnp.dot(a, b, preferred_element_type=F32)


def _dot_t(a, b):
    return lax.dot_general(a, b, (((1,), (1,)), ((), ())), preferred_element_type=F32)


def _rms(x, g):
    ms = jnp.mean(x * x, axis=-1, keepdims=True)
    return x * lax.rsqrt(ms + EPS) * g


def _resident(shape):
    nd = len(shape)
    return pl.BlockSpec(shape, lambda *_: (0,) * nd, pipeline_mode=pl.Buffered(1))


def _layer(arr, layer):
    if layer is None:
        return _resident(arr.shape)
    nd = arr.ndim - 1
    return pl.BlockSpec((None,) + arr.shape[1:], lambda *_: (layer,) + (0,) * nd,
                        pipeline_mode=pl.Buffered(1))


def _cast_plan(sources, grid):
    n_steps = int(np.prod(grid))

    def step_of(idx):
        step = 0
        for n, i in zip(grid, idx):
            step = step * n + i
        return step

    in_specs, out_specs, out_shapes = [], [], []
    for arr, layer in sources:
        _, R, C = arr.shape
        n_blk = n_steps
        while R % n_blk or (R // n_blk) % BF16_SUBLANES:
            n_blk //= 2
        rep = n_steps // n_blk
        in_specs.append(pl.BlockSpec((None, R // n_blk, C),
                                     lambda *idx, layer=layer, rep=rep: (layer, step_of(idx) // rep, 0)))
        out_specs.append(pl.BlockSpec((R // n_blk, C), lambda *idx, rep=rep: (step_of(idx) // rep, 0)))
        out_shapes.append(jax.ShapeDtypeStruct((R, C), BF16))
    return in_specs, out_specs, out_shapes


def _with_casts(body, n_in, n_out, n_cast):
    def wrapped(*refs):
        ins, casts_in = refs[:n_in], refs[n_in:n_in + n_cast]
        outs = refs[n_in + n_cast:n_in + n_cast + n_out]
        casts_out = refs[n_in + n_cast + n_out:n_in + 2 * n_cast + n_out]
        for src, dst in zip(casts_in, casts_out):
            dst[...] = src[...].astype(dst.dtype)
        body(*ins, *outs, *refs[n_in + 2 * n_cast + n_out:])
    return wrapped


def _call_with_casts(body, casts, *, grid, in_specs, out_specs, out_shape, args, name, **kw):
    single = not isinstance(out_shape, (list, tuple))
    out_specs, out_shape = ([out_specs], [out_shape]) if single else (list(out_specs), list(out_shape))
    c_in, c_out, c_shapes = _cast_plan(casts, grid)
    res = pl.pallas_call(
        _with_casts(body, len(in_specs), len(out_specs), len(casts)),
        grid=grid,
        in_specs=list(in_specs) + c_in,
        out_specs=out_specs + c_out,
        out_shape=out_shape + c_shapes,
        compiler_params=pltpu.CompilerParams(dimension_semantics=("arbitrary",) * len(grid),
                                             vmem_limit_bytes=V7X_VMEM_LIMIT_BYTES),
        name=name, **kw,
    )(*args, *[arr for arr, _ in casts])
    outs, conv = res[:len(out_specs)], list(res[len(out_specs):])
    return (outs[0] if single else outs), conv


def _params(n_grid_dims):
    return pltpu.CompilerParams(
        dimension_semantics=("parallel",) * n_grid_dims,
        vmem_limit_bytes=V7X_VMEM_LIMIT_BYTES,
    )


def _ffn_ple_body(h_ref, p_ref, gf_ref, wg_ref, wu_ref, wd_ref, gp_ref, wpg_ref, wpp_ref, gl_ref,
                  *refs, th, n_sub, final_norm, with_attn):
    o_ref = refs[-1]
    ts = h_ref.shape[1] // n_sub
    subs = [slice(k * ts, (k + 1) * ts) for k in range(n_sub)]
    hs = [h_ref[0, rows, :] for rows in subs]
    if with_attn:
        a_ref, wo_ref = refs[:2]
        hs = [h + _dot(a_ref[0, rows, :], wo_ref[...]) for h, rows in zip(hs, subs)]
    xns = [_rms(h, gf_ref[...]).astype(BF16) for h in hs]
    acts = [[] for _ in hs]
    for c in range(wd_ref.shape[0] // th):
        cols = slice(c * th, (c + 1) * th)
        for k, xn in enumerate(xns):
            g = _dot(xn, wg_ref[:, cols])
            u = _dot(xn, wu_ref[:, cols])
            acts[k].append((g * jax.nn.sigmoid(g) * u).astype(BF16))
    accs = [_dot(jnp.concatenate(a, axis=1), wd_ref[...]) for a in acts]
    for h, acc, rows in zip(hs, accs, subs):
        h = h + acc
        gate = jax.nn.sigmoid(_dot(_rms(h, gp_ref[...]).astype(BF16), wpg_ref[...]))
        proj = _dot(p_ref[0, rows, :].astype(BF16), wpp_ref[...])
        h = h + proj * gate
        if final_norm:
            h = _rms(h, gl_ref[...])
        o_ref[0, rows, :] = h


def _ffn_ple(h, p, layer, g_ffn, g_ple, g_last, weights, attn=None, *, final_norm, tm, th, n_sub):
    B, S, D = h.shape
    w_gate, w_up, w_down, w_pg, w_pp = weights
    body = functools.partial(_ffn_ple_body, th=th, n_sub=n_sub, final_norm=final_norm,
                             with_attn=attn is not None)
    tile = lambda w: pl.BlockSpec((1, tm, w), lambda b, i: (b, i, 0))
    in_specs = [tile(D), pl.BlockSpec((None, 1, tm, p.shape[-1]), lambda b, i: (layer, b, i, 0)),
                _layer(g_ffn, layer), _resident(w_gate.shape), _resident(w_up.shape), _resident(w_down.shape),
                _layer(g_ple, layer), _resident(w_pg.shape), _resident(w_pp.shape), _resident(g_last.shape)]
    args = [h, p, g_ffn, w_gate, w_up, w_down, g_ple, w_pg, w_pp, g_last]
    if attn is not None:
        o, w_o = attn
        in_specs += [tile(o.shape[-1]), _resident(w_o.shape)]
        args += [o, w_o]
    return pl.pallas_call(
        body,
        grid=(B, S // tm),
        in_specs=in_specs,
        out_specs=tile(D),
        out_shape=jax.ShapeDtypeStruct((B, S, D), F32),
        compiler_params=_params(2),
        name="ffn_ple",
    )(*args)


def _qkv_body(h_ref, g_ref, w_ref, *refs, tm, n_sub):
    outs, xbuf = refs[:A_GROUPS], refs[A_GROUPS]
    W = A_GROUP_WIDTH
    n_slab = h_ref.shape[2] // LANES
    ts = tm // n_sub
    xns = []
    for k in range(n_sub):
        xn = _rms(h_ref[0, k * ts:(k + 1) * ts, :], g_ref[...])
        for s in range(n_slab):
            xbuf[k, s] = xn[:, s * LANES:(s + 1) * LANES]
        xns.append(xn.astype(BF16))
    for g, d in enumerate(A_DILATIONS):
        for k in range(n_sub):
            if d == 1:
                lhs = xns[k]
            else:
                lhs = jnp.concatenate(
                    [jnp.concatenate([xbuf[k, s, pl.ds(r, ts // d, stride=d), :] for s in range(n_slab)],
                                     axis=1) for r in range(d)], axis=0).astype(BF16)
            for part in range(3):
                c = part * A_GROUPS + g
                y = _dot(lhs, w_ref[:, c * W:(c + 1) * W])
                if part == 0:
                    y = y * (A_HEAD_DIM ** -0.5)
                y = y.astype(BF16)
                for r in range(d):
                    rows_in = slice(r * (ts // d), (r + 1) * (ts // d))
                    rows_out = slice(k * (ts // d), (k + 1) * (ts // d))
                    for pr in range(W // LANES):
                        lo = (3 * pr + part) * LANES
                        outs[g][0, r, rows_out, lo:lo + LANES] = y[rows_in, pr * LANES:(pr + 1) * LANES]


def _qkv_proj(h, g, layer, w, *, tm, n_sub):
    B, S, D = h.shape
    W3 = 3 * A_GROUP_WIDTH
    return pl.pallas_call(
        functools.partial(_qkv_body, tm=tm, n_sub=n_sub),
        grid=(B, S // tm),
        in_specs=[pl.BlockSpec((1, tm, D), lambda b, i: (b, i, 0)), _layer(g, layer), _resident(w.shape)],
        out_specs=[pl.BlockSpec((1, d, tm // d, W3), lambda b, i: (b, 0, i, 0)) for d in A_DILATIONS],
        out_shape=[jax.ShapeDtypeStruct((B, d, S // d, W3), BF16) for d in A_DILATIONS],
        scratch_shapes=[pltpu.VMEM((n_sub, D // LANES, tm // n_sub, LANES), F32)],
        compiler_params=_params(2),
        name="qkv_proj",
    )(h, g, w)


def _phase_major(d):
    return d % 8 == 0


def _phase_pitch(d):
    return A_SPAN // d + 8


def _attn_body(*refs):
    n_in = 2 * A_GROUPS
    bias_ref, out_ref = refs[n_in], refs[n_in + 1]
    bufs = refs[n_in + 2:]
    pv_bufs, m_bufs, den_bufs = bufs[:A_GROUPS], bufs[A_GROUPS:2 * A_GROUPS], bufs[2 * A_GROUPS:]
    first = pl.program_id(1) == 0
    pair = pl.program_id(2)
    head0 = lax.broadcasted_iota(jnp.int32, (1, LANES), 1) < A_HEAD_DIM
    col = lax.broadcasted_iota(jnp.int32, (1, 2 * A_BLOCK), 1)
    no_prev = jnp.where(jnp.logical_and(first, col < A_BLOCK), NEG, 0.0).astype(F32)
    ones = jnp.ones((2 * A_BLOCK, LANES), BF16)
    q_cols, k_cols, v_cols = (slice(part * LANES, (part + 1) * LANES) for part in range(3))

    def token_rows(buf, d, t0, n):
        if not _phase_major(d):
            return buf[pl.ds(t0, n), :]
        return jnp.concatenate([buf[pl.ds(step, d, stride=_phase_pitch(d)), :]
                                for step in range(t0 // d, (t0 + n) // d)], axis=0)

    chunk = 2 * A_BLOCK

    def combine(c):
        group_rows = lambda group_bufs: [token_rows(group_bufs[g], d, c * chunk, chunk)
                                         for g, d in enumerate(A_DILATIONS)]
        ms = group_rows(m_bufs)
        m = functools.reduce(jnp.maximum, ms)
        ws = [jnp.exp(mg - m) for mg in ms]
        num = sum(w * pv for w, pv in zip(ws, group_rows(pv_bufs)))
        den = sum(w * dn for w, dn in zip(ws, group_rows(den_bufs)))
        out_ref[0, pl.ds(c * chunk, chunk), :] = (num / den).astype(BF16)

    for g, d in sorted(enumerate(A_DILATIONS), key=lambda gd: -gd[1]):
        cur_ref, prev_ref = refs[2 * g:2 * g + 2]
        bias2 = bias_ref[g, pl.ds(2 * pair, 2)].reshape(2 * A_BLOCK, 2 * A_BLOCK)
        bias2_first = bias2 + no_prev
        for r in range(d):
            for qb in range(A_SPAN // d // A_BLOCK):
                q = cur_ref[0, r, qb * A_BLOCK:(qb + 1) * A_BLOCK, q_cols]
                if qb == 0:
                    keys = jnp.concatenate([prev_ref[0, r, :, k_cols], cur_ref[0, r, :A_BLOCK, k_cols]], axis=0)
                    vals = jnp.concatenate([prev_ref[0, r, :, v_cols], cur_ref[0, r, :A_BLOCK, v_cols]], axis=0)
                else:
                    keys = cur_ref[0, r, (qb - 1) * A_BLOCK:(qb + 1) * A_BLOCK, k_cols]
                    vals = cur_ref[0, r, (qb - 1) * A_BLOCK:(qb + 1) * A_BLOCK, v_cols]
                zero = jnp.zeros_like(q)
                q2 = jnp.concatenate([jnp.where(head0, q, zero), jnp.where(head0, zero, q)], axis=0)
                s = _dot_t(q2, keys) + (bias2_first if qb == 0 else bias2)
                m = jnp.max(s, axis=-1, keepdims=True)
                e = jnp.exp((s - m).astype(BF16))
                pvd = _dot(e, jnp.concatenate([vals, ones], axis=1))
                if d == 1:
                    rows = pl.ds(qb * A_BLOCK, A_BLOCK)
                elif _phase_major(d):
                    rows = pl.ds(r * _phase_pitch(d) + qb * A_BLOCK, A_BLOCK)
                else:
                    rows = pl.ds(qb * A_BLOCK * d + r, A_BLOCK, stride=d)
                pv_bufs[g][rows, :] = jnp.where(head0, pvd[:A_BLOCK, :LANES], pvd[A_BLOCK:, :LANES])
                den_bufs[g][rows, :] = jnp.where(head0, pvd[:A_BLOCK, LANES:], pvd[A_BLOCK:, LANES:])
                m_bufs[g][rows, :] = jnp.where(head0, m[:A_BLOCK], m[A_BLOCK:])
                if d == 1 and (qb + 1) * A_BLOCK % chunk == 0:
                    combine((qb + 1) * A_BLOCK // chunk - 1)


def _attention(qkv, bias, casts):
    B, _, S, _ = qkv[0].shape
    n_pairs = A_GROUP_WIDTH // LANES
    in_specs, args = [], []
    for g, d in enumerate(A_DILATIONS):
        steps = A_SPAN // d
        bps = steps // A_BLOCK
        in_specs += [
            pl.BlockSpec((1, d, steps, 3 * LANES), lambda b, i, pr: (b, 0, i, pr)),
            pl.BlockSpec((1, d, A_BLOCK, 3 * LANES),
                         lambda b, i, pr, bps=bps: (b, 0, jnp.maximum(i * bps - 1, 0), pr))]
        args += [qkv[g]] * 2
    buf_rows = [d * _phase_pitch(d) if _phase_major(d) else A_SPAN for d in A_DILATIONS]
    return _call_with_casts(
        _attn_body, casts,
        grid=(B, S // A_SPAN, n_pairs),
        in_specs=in_specs + [_resident(bias.shape)],
        out_specs=pl.BlockSpec((1, A_SPAN, LANES), lambda b, i, pr: (b, i, pr)),
        out_shape=jax.ShapeDtypeStruct((B, S, A_GROUP_WIDTH), BF16),
        args=args + [bias],
        scratch_shapes=[pltpu.VMEM((rows, LANES), F32) for rows in buf_rows] * 3,
        name="attention",
    )


def _t5_bucket(n):
    nf = np.maximum(n, 1).astype(np.float32)
    large = REL_MAX_EXACT + (np.log(nf / REL_MAX_EXACT) / np.log(REL_MAX_DIST / REL_MAX_EXACT)
                             * (REL_BUCKETS - REL_MAX_EXACT)).astype(np.int32)
    large = np.minimum(large, REL_BUCKETS - 1)
    return np.where(n < REL_MAX_EXACT, n, large).astype(np.int32)


def _band_buckets():
    i = np.arange(A_BLOCK)[:, None]
    j = np.arange(2 * A_BLOCK)[None, :]
    rel = A_BLOCK + i - j
    valid = (rel >= 0) & (rel <= A_BLOCK)
    return np.stack([np.where(valid, _t5_bucket(np.where(valid, rel, 0) * d), -1)
                     for d in A_DILATIONS]).astype(np.int32)


def _bias_body(table_ref, bucket_ref, o_ref):
    g = pl.program_id(0)
    bk = bucket_ref[0]
    for hd in range(A_HEADS):
        acc = jnp.full(bk.shape, NEG, F32)
        for b in range(REL_BUCKETS):
            acc = jnp.where(bk == b, table_ref[b, g * A_HEADS + hd], acc)
        o_ref[0, hd] = acc


def _band_bias(rel_table):
    buckets = jnp.asarray(_band_buckets())
    G, Q, Kk = buckets.shape
    return pl.pallas_call(
        _bias_body,
        grid=(G,),
        in_specs=[pl.BlockSpec(memory_space=pltpu.SMEM),
                  pl.BlockSpec((1, Q, Kk), lambda g: (g, 0, 0))],
        out_specs=pl.BlockSpec((1, A_HEADS, Q, Kk), lambda g: (g, 0, 0, 0)),
        out_shape=jax.ShapeDtypeStruct((G, A_HEADS, Q, Kk), F32),
        compiler_params=_params(1),
        name="band_bias",
    )(rel_table, buckets)


def _mixer_dilated(h, g_mix, layer, w_qkv, bias, casts, *, tm, n_sub):
    return _attention(_qkv_proj(h, g_mix, layer, w_qkv, tm=tm, n_sub=n_sub), bias, casts)


def _pool_body(h_ref, halo_ref, g_ref, win_ref, wgrp_ref, sc_ref, wout_ref, o_ref, *, tm, n_sub):
    i = pl.program_id(1)
    gain = g_ref[...]
    ts = tm // n_sub
    gw = h_ref.shape[2] // len(B_WINDOWS)
    xn = jnp.concatenate([_rms(halo_ref[0], gain), _rms(h_ref[0], gain)], axis=0).astype(BF16)
    ys = []
    for k in range(n_sub):
        y = _dot(xn[k * ts:(k + 1) * ts + B_HALO], win_ref[...])
        if k == 0:
            y = jnp.concatenate([jnp.where(i == 0, 0.0, y[:B_HALO]), y[B_HALO:]], axis=0)
        ys.append(y)
    pooled = []
    for k, y in enumerate(ys):
        pos = (lax.broadcasted_iota(jnp.int32, (ts, 1), 0) + (i * tm + k * ts + 1)).astype(F32)
        groups = []
        for g, win in enumerate(B_WINDOWS):
            yg = y[:, g * gw:(g + 1) * gw]
            s = yg
            sh = 1
            while sh < win:
                s = s + pltpu.roll(s, sh, axis=0)
                sh *= 2
            mean = s[B_HALO:] * (1.0 / jnp.minimum(pos, float(win)))
            groups.append((mean - yg[B_HALO:]).astype(BF16))
        pooled.append(groups)
    for k, groups in enumerate(pooled):
        z = jnp.concatenate([_dot(pg, wgrp_ref[g * gw:(g + 1) * gw, :]) for g, pg in enumerate(groups)],
                            axis=1) * sc_ref[...]
        rows = slice(k * ts, (k + 1) * ts)
        o_ref[0, rows, :] = h_ref[0, rows, :] + _dot(z.astype(BF16), wout_ref[...])


def _mixer_pool(h, g_mix, layer, weights, scale, j, casts, *, tm, n_sub):
    B, S, D = h.shape
    hpt = tm // B_HALO
    w_in, w_grp, w_out = weights
    return _call_with_casts(
        functools.partial(_pool_body, tm=tm, n_sub=n_sub), casts,
        grid=(B, S // tm),
        in_specs=[pl.BlockSpec((1, tm, D), lambda b, i: (b, i, 0)),
                  pl.BlockSpec((1, B_HALO, D), lambda b, i: (b, jnp.maximum(i * hpt - 1, 0), 0)),
                  _layer(g_mix, layer), _resident(w_in.shape), _resident(w_grp.shape), _layer(scale, j),
                  _resident(w_out.shape)],
        out_specs=pl.BlockSpec((1, tm, D), lambda b, i: (b, i, 0)),
        out_shape=jax.ShapeDtypeStruct((B, S, D), F32),
        args=[h, h, g_mix, w_in, w_grp, scale, w_out],
        name="mixer_pool",
    )


def _sgu_body(h_ref, g_ref, win_ref, vg_ref, ws_ref, bs_ref, wout_ref, o_ref, *, tm, n_sub):
    D = h_ref.shape[2]
    gw = D // C_GROUPS
    t = lax.broadcasted_iota(jnp.int32, (C_CHUNK, C_CHUNK), 0)
    s = lax.broadcasted_iota(jnp.int32, (C_CHUNK, C_CHUNK), 1)
    wms = [jnp.where(s <= t, ws_ref[g], 0.0).astype(BF16) for g in range(C_GROUPS)]
    ts = tm // n_sub
    subs = [slice(k * ts, (k + 1) * ts) for k in range(n_sub)]
    xns = [_rms(h_ref[0, rows, :], g_ref[...]).astype(BF16) for rows in subs]
    zs = [(_dot(xn, win_ref[:, D:]), _dot(xn, win_ref[:, :D])) for xn in xns]
    uvs = []
    for zv, zu in zs:
        v = jax.nn.gelu(zv)
        mu = jnp.mean(v, axis=-1, keepdims=True)
        var = jnp.mean(jnp.square(v - mu), axis=-1, keepdims=True)
        vn = ((v - mu) * lax.rsqrt(var + EPS) * vg_ref[...]).astype(BF16)
        uvs.append((jax.nn.gelu(zu), vn))
    for (u, vn), rows in zip(uvs, subs):
        cols = []
        for g in range(C_GROUPS):
            b = bs_ref[:, g:g + 1]
            cols.append(jnp.concatenate(
                [_dot(wms[g], vn[n * C_CHUNK:(n + 1) * C_CHUNK, g * gw:(g + 1) * gw]) + b
                 for n in range(ts // C_CHUNK)], axis=0))
        sp = jnp.concatenate(cols, axis=1)
        o_ref[0, rows, :] = h_ref[0, rows, :] + _dot((u * sp).astype(BF16), wout_ref[...])


def _mixer_sgu(h, g_mix, layer, weights, v_gain, w_s, b_s_t, j, casts, *, tm, n_sub):
    B, S, D = h.shape
    w_in, w_out = weights
    return _call_with_casts(
        functools.partial(_sgu_body, tm=tm, n_sub=n_sub), casts,
        grid=(B, S // tm),
        in_specs=[pl.BlockSpec((1, tm, D), lambda b, i: (b, i, 0)),
                  _layer(g_mix, layer), _resident(w_in.shape), _layer(v_gain, j),
                  _layer(w_s, j), _layer(b_s_t, j), _resident(w_out.shape)],
        out_specs=pl.BlockSpec((1, tm, D), lambda b, i: (b, i, 0)),
        out_shape=jax.ShapeDtypeStruct((B, S, D), F32),
        args=[h, g_mix, w_in, v_gain, w_s, b_s_t, w_out],
        name="mixer_sgu",
    )


def kernel(x, p, rel_table, norm_mix, norm_ffn, norm_ple, final_norm, a_w_qkv, a_w_o, b_w_in, b_w_grp, b_scale, b_w_out, c_w_in, c_v_gain, c_w_s, c_b_s, c_w_out, ffn_w_gate, ffn_w_up, ffn_w_down, ple_w_gate, ple_w_proj):
    depth = norm_mix.shape[0]
    rows = lambda v: v[..., None, :]
    tiles = dict(tm=TILE_ROWS, n_sub=SUB_TILES)
    g_mix, g_ffn, g_ple, g_last = rows(norm_mix), rows(norm_ffn), rows(norm_ple), rows(final_norm)
    b_w_grp = b_w_grp.reshape(b_w_grp.shape[0], -1, b_w_grp.shape[-1])

    def mixer_sources(i):
        kind, j = i % 3, i // 3
        stacks = {0: (a_w_qkv, a_w_o), 1: (b_w_in, b_w_grp, b_w_out), 2: (c_w_in, c_w_out)}[kind]
        return [(w, j) for w in stacks]

    ffn_sources = lambda i: [(w, i) for w in (ffn_w_gate, ffn_w_up, ffn_w_down, ple_w_gate, ple_w_proj)]
    mixer_w = [w[j].astype(BF16) for w, j in mixer_sources(0)]
    bias = _band_bias(rel_table)
    h = x
    for i in range(depth):
        kind, j = i % 3, i // 3
        casts = ffn_sources(i) + (mixer_sources(i + 1) if i + 1 < depth else [])
        attn = None
        if kind == 0:
            o, conv = _mixer_dilated(h, g_mix, i, mixer_w[0], bias, casts, **tiles)
            attn = (o, mixer_w[1])
        elif kind == 1:
            h, conv = _mixer_pool(h, g_mix, i, mixer_w, rows(b_scale), j, casts, **tiles)
        else:
            h, conv = _mixer_sgu(h, g_mix, i, mixer_w, rows(c_v_gain), c_w_s, jnp.swapaxes(c_b_s, 1, 2), j,
                                 casts, **tiles)
        ffn_w, mixer_w = conv[:5], conv[5:]
        h = _ffn_ple(h, p, i, g_ffn, g_ple, g_last, ffn_w, attn, final_norm=(i == depth - 1),
                     th=FFN_CHUNK, **tiles)
    return h
```

```python
import functools

import jax
import jax.numpy as jnp
import numpy as np
from jax import lax
from jax.experimental import pallas as pl
from jax.experimental.pallas import tpu as pltpu

EPS = 1e-6
NEG = -1e30

A_DILATIONS = (1, 4, 16)
A_GROUPS = 3
A_HEADS = 8
A_HEAD_DIM = 64
A_GROUP_WIDTH = A_HEADS * A_HEAD_DIM
A_BLOCK = 128
A_SPAN = A_BLOCK * max(A_DILATIONS)
REL_BUCKETS = 32
REL_MAX_EXACT = REL_BUCKETS // 2
REL_MAX_DIST = 2048
B_WINDOWS = (2, 4, 8, 16)
B_HALO = 16
C_CHUNK = 128
C_GROUPS = 4

LANES = 128
BF16_SUBLANES = 16
V7X_MXU_WIDTH = 256
V7X_VMEM_LIMIT_BYTES = 56 * 1024 * 1024

TILE_ROWS = 1024
SUB_TILES = 2
FFN_CHUNK = V7X_MXU_WIDTH

BF16 = jnp.bfloat16
F32 = jnp.float32


def _dot(a, b):
    return jnp.dot(a, b, preferred_element_type=F32)


def _dot_t(a, b):
    return lax.dot_general(a, b, (((1,), (1,)), ((), ())), preferred_element_type=F32)


def _rms(x, g):
    ms = jnp.mean(x * x, axis=-1, keepdims=True)
    return x * lax.rsqrt(ms + EPS) * g


def _resident(shape):
    nd = len(shape)
    return pl.BlockSpec(shape, lambda *_: (0,) * nd, pipeline_mode=pl.Buffered(1))


def _layer(arr, layer):
    if layer is None:
        return _resident(arr.shape)
    nd = arr.ndim - 1
    return pl.BlockSpec((None,) + arr.shape[1:], lambda *_: (layer,) + (0,) * nd,
                        pipeline_mode=pl.Buffered(1))


def _cast_plan(sources, grid):
    n_steps = int(np.prod(grid))

    def step_of(idx):
        step = 0
        for n, i in zip(grid, idx):
            step = step * n + i
        return step

    in_specs, out_specs, out_shapes = [], [], []
    for arr, layer in sources:
        _, R, C = arr.shape
        n_blk = n_steps
        while R % n_blk or (R // n_blk) % BF16_SUBLANES:
            n_blk //= 2
        rep = n_steps // n_blk
        in_specs.append(pl.BlockSpec((None, R // n_blk, C),
                                     lambda *idx, layer=layer, rep=rep: (layer, step_of(idx) // rep, 0)))
        out_specs.append(pl.BlockSpec((R // n_blk, C), lambda *idx, rep=rep: (step_of(idx) // rep, 0)))
        out_shapes.append(jax.ShapeDtypeStruct((R, C), BF16))
    return in_specs, out_specs, out_shapes


def _with_casts(body, n_in, n_out, n_cast):
    def wrapped(*refs):
        ins, casts_in = refs[:n_in], refs[n_in:n_in + n_cast]
        outs = refs[n_in + n_cast:n_in + n_cast + n_out]
        casts_out = refs[n_in + n_cast + n_out:n_in + 2 * n_cast + n_out]
        for src, dst in zip(casts_in, casts_out):
            dst[...] = src[...].astype(dst.dtype)
        body(*ins, *outs, *refs[n_in + 2 * n_cast + n_out:])
    return wrapped


def _call_with_casts(body, casts, *, grid, in_specs, out_specs, out_shape, args, name, **kw):
    single = not isinstance(out_shape, (list, tuple))
    out_specs, out_shape = ([out_specs], [out_shape]) if single else (list(out_specs), list(out_shape))
    c_in, c_out, c_shapes = _cast_plan(casts, grid)
    res = pl.pallas_call(
        _with_casts(body, len(in_specs), len(out_specs), len(casts)),
        grid=grid,
        in_specs=list(in_specs) + c_in,
        out_specs=out_specs + c_out,
        out_shape=out_shape + c_shapes,
        compiler_params=pltpu.CompilerParams(dimension_semantics=("arbitrary",) * len(grid),
                                             vmem_limit_bytes=V7X_VMEM_LIMIT_BYTES),
        name=name, **kw,
    )(*args, *[arr for arr, _ in casts])
    outs, conv = res[:len(out_specs)], list(res[len(out_specs):])
    return (outs[0] if single else outs), conv


def _params(n_grid_dims):
    return pltpu.CompilerParams(
        dimension_semantics=("parallel",) * n_grid_dims,
        vmem_limit_bytes=V7X_VMEM_LIMIT_BYTES,
    )


def _ffn_ple_body(h_ref, p_ref, gf_ref, wg_ref, wu_ref, wd_ref, gp_ref, wpg_ref, wpp_ref, gl_ref,
                  *refs, th, n_sub, final_norm, with_attn):
    o_ref = refs[-1]
    ts = h_ref.shape[1] // n_sub
    subs = [slice(k * ts, (k + 1) * ts) for k in range(n_sub)]
    hs = [h_ref[0, rows, :] for rows in subs]
    if with_attn:
        a_ref, wo_ref = refs[:2]
        hs = [h + _dot(a_ref[0, rows, :], wo_ref[...]) for h, rows in zip(hs, subs)]
    xns = [_rms(h, gf_ref[...]).astype(BF16) for h in hs]
    acts = [[] for _ in hs]
    for c in range(wd_ref.shape[0] // th):
        cols = slice(c * th, (c + 1) * th)
        for k, xn in enumerate(xns):
            g = _dot(xn, wg_ref[:, cols])
            u = _dot(xn, wu_ref[:, cols])
            acts[k].append((g * jax.nn.sigmoid(g) * u).astype(BF16))
    accs = [_dot(jnp.concatenate(a, axis=1), wd_ref[...]) for a in acts]
    for h, acc, rows in zip(hs, accs, subs):
        h = h + acc
        gate = jax.nn.sigmoid(_dot(_rms(h, gp_ref[...]).astype(BF16), wpg_ref[...]))
        proj = _dot(p_ref[0, rows, :].astype(BF16), wpp_ref[...])
        h = h + proj * gate
        if final_norm:
            h = _rms(h, gl_ref[...])
        o_ref[0, rows, :] = h


def _ffn_ple(h, p, layer, g_ffn, g_ple, g_last, weights, attn=None, *, final_norm, tm, th, n_sub):
    B, S, D = h.shape
    w_gate, w_up, w_down, w_pg, w_pp = weights
    body = functools.partial(_ffn_ple_body, th=th, n_sub=n_sub, final_norm=final_norm,
                             with_attn=attn is not None)
    tile = lambda w: pl.BlockSpec((1, tm, w), lambda b, i: (b, i, 0))
    in_specs = [tile(D), pl.BlockSpec((None, 1, tm, p.shape[-1]), lambda b, i: (layer, b, i, 0)),
                _layer(g_ffn, layer), _resident(w_gate.shape), _resident(w_up.shape), _resident(w_down.shape),
                _layer(g_ple, layer), _resident(w_pg.shape), _resident(w_pp.shape), _resident(g_last.shape)]
    args = [h, p, g_ffn, w_gate, w_up, w_down, g_ple, w_pg, w_pp, g_last]
    if attn is not None:
        o, w_o = attn
        in_specs += [tile(o.shape[-1]), _resident(w_o.shape)]
        args += [o, w_o]
    return pl.pallas_call(
        body,
        grid=(B, S // tm),
        in_specs=in_specs,
        out_specs=tile(D),
        out_shape=jax.ShapeDtypeStruct((B, S, D), F32),
        compiler_params=_params(2),
        name="ffn_ple",
    )(*args)


def _qkv_body(h_ref, g_ref, w_ref, *refs, tm, n_sub):
    outs, xbuf = refs[:A_GROUPS], refs[A_GROUPS]
    W = A_GROUP_WIDTH
    n_slab = h_ref.shape[2] // LANES
    ts = tm // n_sub
    xns = []
    for k in range(n_sub):
        xn = _rms(h_ref[0, k * ts:(k + 1) * ts, :], g_ref[...])
        for s in range(n_slab):
            xbuf[k, s] = xn[:, s * LANES:(s + 1) * LANES]
        xns.append(xn.astype(BF16))
    for g, d in enumerate(A_DILATIONS):
        for k in range(n_sub):
            if d == 1:
                lhs = xns[k]
            else:
                lhs = jnp.concatenate(
                    [jnp.concatenate([xbuf[k, s, pl.ds(r, ts // d, stride=d), :] for s in range(n_slab)],
                                     axis=1) for r in range(d)], axis=0).astype(BF16)
            for part in range(3):
                c = part * A_GROUPS + g
                y = _dot(lhs, w_ref[:, c * W:(c + 1) * W])
                if part == 0:
                    y = y * (A_HEAD_DIM ** -0.5)
                y = y.astype(BF16)
                for r in range(d):
                    rows_in = slice(r * (ts // d), (r + 1) * (ts // d))
                    rows_out = slice(k * (ts // d), (k + 1) * (ts // d))
                    for pr in range(W // LANES):
                        lo = (3 * pr + part) * LANES
                        outs[g][0, r, rows_out, lo:lo + LANES] = y[rows_in, pr * LANES:(pr + 1) * LANES]


def _qkv_proj(h, g, layer, w, casts, *, tm, n_sub):
    B, S, D = h.shape
    W3 = 3 * A_GROUP_WIDTH
    return _call_with_casts(
        functools.partial(_qkv_body, tm=tm, n_sub=n_sub), casts,
        grid=(B, S // tm),
        in_specs=[pl.BlockSpec((1, tm, D), lambda b, i: (b, i, 0)), _layer(g, layer), _resident(w.shape)],
        out_specs=[pl.BlockSpec((1, d, tm // d, W3), lambda b, i: (b, 0, i, 0)) for d in A_DILATIONS],
        out_shape=[jax.ShapeDtypeStruct((B, d, S // d, W3), BF16) for d in A_DILATIONS],
        args=[h, g, w],
        scratch_shapes=[pltpu.VMEM((n_sub, D // LANES, tm // n_sub, LANES), F32)],
        name="qkv_proj",
    )


def _phase_major(d):
    return d % 8 == 0


def _phase_pitch(d):
    return A_SPAN // d + 8


def _attn_body(*refs):
    n_in = 2 * A_GROUPS
    bias_ref, out_ref = refs[n_in], refs[n_in + 1]
    bufs = refs[n_in + 2:]
    pv_bufs, m_bufs, den_bufs = bufs[:A_GROUPS], bufs[A_GROUPS:2 * A_GROUPS], bufs[2 * A_GROUPS:]
    first = pl.program_id(1) == 0
    pair = pl.program_id(2)
    head0 = lax.broadcasted_iota(jnp.int32, (1, LANES), 1) < A_HEAD_DIM
    col = lax.broadcasted_iota(jnp.int32, (1, 2 * A_BLOCK), 1)
    no_prev = jnp.where(jnp.logical_and(first, col < A_BLOCK), NEG, 0.0).astype(F32)
    ones = jnp.ones((2 * A_BLOCK, LANES), BF16)
    q_cols, k_cols, v_cols = (slice(part * LANES, (part + 1) * LANES) for part in range(3))

    def token_rows(buf, d, t0, n):
        if not _phase_major(d):
            return buf[pl.ds(t0, n), :]
        return jnp.concatenate([buf[pl.ds(step, d, stride=_phase_pitch(d)), :]
                                for step in range(t0 // d, (t0 + n) // d)], axis=0)

    chunk = 2 * A_BLOCK

    def combine(c):
        group_rows = lambda group_bufs: [token_rows(group_bufs[g], d, c * chunk, chunk)
                                         for g, d in enumerate(A_DILATIONS)]
        ms = group_rows(m_bufs)
        m = functools.reduce(jnp.maximum, ms)
        ws = [jnp.exp(mg - m) for mg in ms]
        num = sum(w * pv for w, pv in zip(ws, group_rows(pv_bufs)))
        den = sum(w * dn for w, dn in zip(ws, group_rows(den_bufs)))
        out_ref[0, pl.ds(c * chunk, chunk), :] = (num / den).astype(BF16)

    for g, d in sorted(enumerate(A_DILATIONS), key=lambda gd: -gd[1]):
        cur_ref, prev_ref = refs[2 * g:2 * g + 2]
        bias2 = bias_ref[g, pl.ds(2 * pair, 2)].reshape(2 * A_BLOCK, 2 * A_BLOCK)
        bias2_first = bias2 + no_prev
        for r in range(d):
            for qb in range(A_SPAN // d // A_BLOCK):
                q = cur_ref[0, r, qb * A_BLOCK:(qb + 1) * A_BLOCK, q_cols]
                if qb == 0:
                    keys = jnp.concatenate([prev_ref[0, r, :, k_cols], cur_ref[0, r, :A_BLOCK, k_cols]], axis=0)
                    vals = jnp.concatenate([prev_ref[0, r, :, v_cols], cur_ref[0, r, :A_BLOCK, v_cols]], axis=0)
                else:
                    keys = cur_ref[0, r, (qb - 1) * A_BLOCK:(qb + 1) * A_BLOCK, k_cols]
                    vals = cur_ref[0, r, (qb - 1) * A_BLOCK:(qb + 1) * A_BLOCK, v_cols]
                zero = jnp.zeros_like(q)
                q2 = jnp.concatenate([jnp.where(head0, q, zero), jnp.where(head0, zero, q)], axis=0)
                s = _dot_t(q2, keys) + (bias2_first if qb == 0 else bias2)
                m = jnp.max(s, axis=-1, keepdims=True)
                e = jnp.exp((s - m).astype(BF16))
                pvd = _dot(e, jnp.concatenate([vals, ones], axis=1))
                if d == 1:
                    rows = pl.ds(qb * A_BLOCK, A_BLOCK)
                elif _phase_major(d):
                    rows = pl.ds(r * _phase_pitch(d) + qb * A_BLOCK, A_BLOCK)
                else:
                    rows = pl.ds(qb * A_BLOCK * d + r, A_BLOCK, stride=d)
                pv_bufs[g][rows, :] = jnp.where(head0, pvd[:A_BLOCK, :LANES], pvd[A_BLOCK:, :LANES])
                den_bufs[g][rows, :] = jnp.where(head0, pvd[:A_BLOCK, LANES:], pvd[A_BLOCK:, LANES:])
                m_bufs[g][rows, :] = jnp.where(head0, m[:A_BLOCK], m[A_BLOCK:])
                if d == 1 and (qb + 1) * A_BLOCK % chunk == 0:
                    combine((qb + 1) * A_BLOCK // chunk - 1)


def _attention(qkv, bias):
    B, _, S, _ = qkv[0].shape
    n_pairs = A_GROUP_WIDTH // LANES
    in_specs, args = [], []
    for g, d in enumerate(A_DILATIONS):
        steps = A_SPAN // d
        bps = steps // A_BLOCK
        in_specs += [
            pl.BlockSpec((1, d, steps, 3 * LANES), lambda b, i, pr: (b, 0, i, pr)),
            pl.BlockSpec((1, d, A_BLOCK, 3 * LANES),
                         lambda b, i, pr, bps=bps: (b, 0, jnp.maximum(i * bps - 1, 0), pr))]
        args += [qkv[g]] * 2
    buf_rows = [d * _phase_pitch(d) if _phase_major(d) else A_SPAN for d in A_DILATIONS]
    return pl.pallas_call(
        _attn_body,
        grid=(B, S // A_SPAN, n_pairs),
        in_specs=in_specs + [_resident(bias.shape)],
        out_specs=pl.BlockSpec((1, A_SPAN, LANES), lambda b, i, pr: (b, i, pr)),
        out_shape=jax.ShapeDtypeStruct((B, S, A_GROUP_WIDTH), BF16),
        scratch_shapes=[pltpu.VMEM((rows, LANES), F32) for rows in buf_rows] * 3,
        compiler_params=_params(3),
        name="attention",
    )(*args, bias)


def _t5_bucket(n):
    nf = np.maximum(n, 1).astype(np.float32)
    large = REL_MAX_EXACT + (np.log(nf / REL_MAX_EXACT) / np.log(REL_MAX_DIST / REL_MAX_EXACT)
                             * (REL_BUCKETS - REL_MAX_EXACT)).astype(np.int32)
    large = np.minimum(large, REL_BUCKETS - 1)
    return np.where(n < REL_MAX_EXACT, n, large).astype(np.int32)


def _band_buckets():
    i = np.arange(A_BLOCK)[:, None]
    j = np.arange(2 * A_BLOCK)[None, :]
    rel = A_BLOCK + i - j
    valid = (rel >= 0) & (rel <= A_BLOCK)
    return np.stack([np.where(valid, _t5_bucket(np.where(valid, rel, 0) * d), -1)
                     for d in A_DILATIONS]).astype(np.int32)


def _bias_body(table_ref, bucket_ref, o_ref):
    g = pl.program_id(0)
    bk = bucket_ref[0]
    for hd in range(A_HEADS):
        acc = jnp.full(bk.shape, NEG, F32)
        for b in range(REL_BUCKETS):
            acc = jnp.where(bk == b, table_ref[b, g * A_HEADS + hd], acc)
        o_ref[0, hd] = acc


def _band_bias(rel_table):
    buckets = jnp.asarray(_band_buckets())
    G, Q, Kk = buckets.shape
    return pl.pallas_call(
        _bias_body,
        grid=(G,),
        in_specs=[pl.BlockSpec(memory_space=pltpu.SMEM),
                  pl.BlockSpec((1, Q, Kk), lambda g: (g, 0, 0))],
        out_specs=pl.BlockSpec((1, A_HEADS, Q, Kk), lambda g: (g, 0, 0, 0)),
        out_shape=jax.ShapeDtypeStruct((G, A_HEADS, Q, Kk), F32),
        compiler_params=_params(1),
        name="band_bias",
    )(rel_table, buckets)


def _mixer_dilated(h, g_mix, layer, w_qkv, bias, casts, *, tm, n_sub):
    qkv, conv = _qkv_proj(h, g_mix, layer, w_qkv, casts, tm=tm, n_sub=n_sub)
    return _attention(qkv, bias), conv


def _pool_body(h_ref, halo_ref, g_ref, win_ref, wgrp_ref, sc_ref, wout_ref, o_ref, *, tm, n_sub):
    i = pl.program_id(1)
    gain = g_ref[...]
    ts = tm // n_sub
    gw = h_ref.shape[2] // len(B_WINDOWS)
    xn = jnp.concatenate([_rms(halo_ref[0], gain), _rms(h_ref[0], gain)], axis=0).astype(BF16)
    ys = []
    for k in range(n_sub):
        y = _dot(xn[k * ts:(k + 1) * ts + B_HALO], win_ref[...])
        if k == 0:
            y = jnp.concatenate([jnp.where(i == 0, 0.0, y[:B_HALO]), y[B_HALO:]], axis=0)
        ys.append(y)
    pooled = []
    for k, y in enumerate(ys):
        pos = (lax.broadcasted_iota(jnp.int32, (ts, 1), 0) + (i * tm + k * ts + 1)).astype(F32)
        groups = []
        for g, win in enumerate(B_WINDOWS):
            yg = y[:, g * gw:(g + 1) * gw]
            s = yg
            sh = 1
            while sh < win:
                s = s + pltpu.roll(s, sh, axis=0)
                sh *= 2
            mean = s[B_HALO:] * (1.0 / jnp.minimum(pos, float(win)))
            groups.append((mean - yg[B_HALO:]).astype(BF16))
        pooled.append(groups)
    for k, groups in enumerate(pooled):
        z = jnp.concatenate([_dot(pg, wgrp_ref[g * gw:(g + 1) * gw, :]) for g, pg in enumerate(groups)],
                            axis=1) * sc_ref[...]
        rows = slice(k * ts, (k + 1) * ts)
        o_ref[0, rows, :] = h_ref[0, rows, :] + _dot(z.astype(BF16), wout_ref[...])


def _mixer_pool(h, g_mix, layer, weights, scale, j, casts, *, tm, n_sub):
    B, S, D = h.shape
    hpt = tm // B_HALO
    w_in, w_grp, w_out = weights
    return _call_with_casts(
        functools.partial(_pool_body, tm=tm, n_sub=n_sub), casts,
        grid=(B, S // tm),
        in_specs=[pl.BlockSpec((1, tm, D), lambda b, i: (b, i, 0)),
                  pl.BlockSpec((1, B_HALO, D), lambda b, i: (b, jnp.maximum(i * hpt - 1, 0), 0)),
                  _layer(g_mix, layer), _resident(w_in.shape), _resident(w_grp.shape), _layer(scale, j),
                  _resident(w_out.shape)],
        out_specs=pl.BlockSpec((1, tm, D), lambda b, i: (b, i, 0)),
        out_shape=jax.ShapeDtypeStruct((B, S, D), F32),
        args=[h, h, g_mix, w_in, w_grp, scale, w_out],
        name="mixer_pool",
    )


def _sgu_body(h_ref, g_ref, win_ref, vg_ref, ws_ref, bs_ref, wout_ref, o_ref, *, tm, n_sub):
    D = h_ref.shape[2]
    gw = D // C_GROUPS
    t = lax.broadcasted_iota(jnp.int32, (C_CHUNK, C_CHUNK), 0)
    s = lax.broadcasted_iota(jnp.int32, (C_CHUNK, C_CHUNK), 1)
    wms = [jnp.where(s <= t, ws_ref[g], 0.0).astype(BF16) for g in range(C_GROUPS)]
    ts = tm // n_sub
    subs = [slice(k * ts, (k + 1) * ts) for k in range(n_sub)]
    xns = [_rms(h_ref[0, rows, :], g_ref[...]).astype(BF16) for rows in subs]
    zs = [(_dot(xn, win_ref[:, D:]), _dot(xn, win_ref[:, :D])) for xn in xns]
    uvs = []
    for zv, zu in zs:
        v = jax.nn.gelu(zv)
        mu = jnp.mean(v, axis=-1, keepdims=True)
        var = jnp.mean(jnp.square(v - mu), axis=-1, keepdims=True)
        vn = ((v - mu) * lax.rsqrt(var + EPS) * vg_ref[...]).astype(BF16)
        uvs.append((jax.nn.gelu(zu), vn))
    for (u, vn), rows in zip(uvs, subs):
        cols = []
        for g in range(C_GROUPS):
            b = bs_ref[:, g:g + 1]
            cols.append(jnp.concatenate(
                [_dot(wms[g], vn[n * C_CHUNK:(n + 1) * C_CHUNK, g * gw:(g + 1) * gw]) + b
                 for n in range(ts // C_CHUNK)], axis=0))
        sp = jnp.concatenate(cols, axis=1)
        o_ref[0, rows, :] = h_ref[0, rows, :] + _dot((u * sp).astype(BF16), wout_ref[...])


def _mixer_sgu(h, g_mix, layer, weights, v_gain, w_s, b_s_t, j, casts, *, tm, n_sub):
    B, S, D = h.shape
    w_in, w_out = weights
    return _call_with_casts(
        functools.partial(_sgu_body, tm=tm, n_sub=n_sub), casts,
        grid=(B, S // tm),
        in_specs=[pl.BlockSpec((1, tm, D), lambda b, i: (b, i, 0)),
                  _layer(g_mix, layer), _resident(w_in.shape), _layer(v_gain, j),
                  _layer(w_s, j), _layer(b_s_t, j), _resident(w_out.shape)],
        out_specs=pl.BlockSpec((1, tm, D), lambda b, i: (b, i, 0)),
        out_shape=jax.ShapeDtypeStruct((B, S, D), F32),
        args=[h, g_mix, w_in, v_gain, w_s, b_s_t, w_out],
        name="mixer_sgu",
    )


def kernel(x, p, rel_table, norm_mix, norm_ffn, norm_ple, final_norm, a_w_qkv, a_w_o, b_w_in, b_w_grp, b_scale, b_w_out, c_w_in, c_v_gain, c_w_s, c_b_s, c_w_out, ffn_w_gate, ffn_w_up, ffn_w_down, ple_w_gate, ple_w_proj):
    depth = norm_mix.shape[0]
    rows = lambda v: v[..., None, :]
    tiles = dict(tm=TILE_ROWS, n_sub=SUB_TILES)
    g_mix, g_ffn, g_ple, g_last = rows(norm_mix), rows(norm_ffn), rows(norm_ple), rows(final_norm)
    b_w_grp = b_w_grp.reshape(b_w_grp.shape[0], -1, b_w_grp.shape[-1])

    def mixer_sources(i):
        kind, j = i % 3, i // 3
        stacks = {0: (a_w_qkv, a_w_o), 1: (b_w_in, b_w_grp, b_w_out), 2: (c_w_in, c_w_out)}[kind]
        return [(w, j) for w in stacks]

    ffn_sources = lambda i: [(w, i) for w in (ffn_w_gate, ffn_w_up, ffn_w_down, ple_w_gate, ple_w_proj)]
    mixer_w = [w[j].astype(BF16) for w, j in mixer_sources(0)]
    bias = _band_bias(rel_table)
    h = x
    for i in range(depth):
        kind, j = i % 3, i // 3
        casts = ffn_sources(i) + (mixer_sources(i + 1) if i + 1 < depth else [])
        attn = None
        if kind == 0:
            o, conv = _mixer_dilated(h, g_mix, i, mixer_w[0], bias, casts, **tiles)
            attn = (o, mixer_w[1])
        elif kind == 1:
            h, conv = _mixer_pool(h, g_mix, i, mixer_w, rows(b_scale), j, casts, **tiles)
        else:
            h, conv = _mixer_sgu(h, g_mix, i, mixer_w, rows(c_v_gain), c_w_s, jnp.swapaxes(c_b_s, 1, 2), j,
                                 casts, **tiles)
        ffn_w, mixer_w = conv[:5], conv[5:]
        h = _ffn_ple(h, p, i, g_ffn, g_ple, g_last, ffn_w, attn, final_norm=(i == depth - 1),
                     th=FFN_CHUNK, **tiles)
    return h
```

```python
import functools

import jax
import jax.numpy as jnp
import numpy as np
from jax import lax
from jax.experimental import pallas as pl
from jax.experimental.pallas import tpu as pltpu

EPS = 1e-6
NEG = -1e30

A_DILATIONS = (1, 4, 16)
A_GROUPS = 3
A_HEADS = 8
A_HEAD_DIM = 64
A_GROUP_WIDTH = A_HEADS * A_HEAD_DIM
A_BLOCK = 128
A_SPAN = A_BLOCK * max(A_DILATIONS)
REL_BUCKETS = 32
REL_MAX_EXACT = REL_BUCKETS // 2
REL_MAX_DIST = 2048
B_WINDOWS = (2, 4, 8, 16)
B_HALO = 16
C_CHUNK = 128
C_GROUPS = 4

LANES = 128
BF16_SUBLANES = 16
V7X_MXU_WIDTH = 256
V7X_VMEM_LIMIT_BYTES = 56 * 1024 * 1024

TILE_ROWS = 1024
SUB_TILES = 2
FFN_CHUNK = V7X_MXU_WIDTH

BF16 = jnp.bfloat16
F32 = jnp.float32


def _dot(a, b):
    return jnp.dot(a, b, preferred_element_type=F32)


def _dot_t(a, b):
    return lax.dot_general(a, b, (((1,), (1,)), ((), ())), preferred_element_type=F32)


def _rms(x, g):
    ms = jnp.mean(x * x, axis=-1, keepdims=True)
    return x * lax.rsqrt(ms + EPS) * g


def _resident(shape):
    nd = len(shape)
    return pl.BlockSpec(shape, lambda *_: (0,) * nd, pipeline_mode=pl.Buffered(1))


def _layer(arr, layer):
    if layer is None:
        return _resident(arr.shape)
    nd = arr.ndim - 1
    return pl.BlockSpec((None,) + arr.shape[1:], lambda *_: (layer,) + (0,) * nd,
                        pipeline_mode=pl.Buffered(1))


def _cast_plan(sources, grid):
    n_steps = int(np.prod(grid))

    def step_of(idx):
        step = 0
        for n, i in zip(grid, idx):
            step = step * n + i
        return step

    in_specs, out_specs, out_shapes = [], [], []
    for arr, layer in sources:
        _, R, C = arr.shape
        n_blk = n_steps
        while R % n_blk or (R // n_blk) % BF16_SUBLANES:
            n_blk //= 2
        rep = n_steps // n_blk
        in_specs.append(pl.BlockSpec((None, R // n_blk, C),
                                     lambda *idx, layer=layer, rep=rep: (layer, step_of(idx) // rep, 0)))
        out_specs.append(pl.BlockSpec((R // n_blk, C), lambda *idx, rep=rep: (step_of(idx) // rep, 0)))
        out_shapes.append(jax.ShapeDtypeStruct((R, C), BF16))
    return in_specs, out_specs, out_shapes


def _with_casts(body, n_in, n_out, n_cast):
    def wrapped(*refs):
        ins, casts_in = refs[:n_in], refs[n_in:n_in + n_cast]
        outs = refs[n_in + n_cast:n_in + n_cast + n_out]
        casts_out = refs[n_in + n_cast + n_out:n_in + 2 * n_cast + n_out]
        for src, dst in zip(casts_in, casts_out):
            dst[...] = src[...].astype(dst.dtype)
        body(*ins, *outs, *refs[n_in + 2 * n_cast + n_out:])
    return wrapped


def _call_with_casts(body, casts, *, grid, in_specs, out_specs, out_shape, args, name, **kw):
    single = not isinstance(out_shape, (list, tuple))
    out_specs, out_shape = ([out_specs], [out_shape]) if single else (list(out_specs), list(out_shape))
    c_in, c_out, c_shapes = _cast_plan(casts, grid)
    res = pl.pallas_call(
        _with_casts(body, len(in_specs), len(out_specs), len(casts)),
        grid=grid,
        in_specs=list(in_specs) + c_in,
        out_specs=out_specs + c_out,
        out_shape=out_shape + c_shapes,
        compiler_params=pltpu.CompilerParams(dimension_semantics=("arbitrary",) * len(grid),
                                             vmem_limit_bytes=V7X_VMEM_LIMIT_BYTES),
        name=name, **kw,
    )(*args, *[arr for arr, _ in casts])
    outs, conv = res[:len(out_specs)], list(res[len(out_specs):])
    return (outs[0] if single else outs), conv


def _params(n_grid_dims):
    return pltpu.CompilerParams(
        dimension_semantics=("parallel",) * n_grid_dims,
        vmem_limit_bytes=V7X_VMEM_LIMIT_BYTES,
    )


def _ffn_ple_body(h_ref, p_ref, gf_ref, wg_ref, wu_ref, wd_ref, gp_ref, wpg_ref, wpp_ref, gl_ref,
                  *refs, th, n_sub, final_norm, with_attn):
    o_ref = refs[-1]
    ts = h_ref.shape[1] // n_sub
    subs = [slice(k * ts, (k + 1) * ts) for k in range(n_sub)]
    hs = [h_ref[0, rows, :] for rows in subs]
    if with_attn:
        a_ref, wo_ref = refs[:2]
        hs = [h + _dot(a_ref[0, rows, :], wo_ref[...]) for h, rows in zip(hs, subs)]
    xns = [_rms(h, gf_ref[...]).astype(BF16) for h in hs]
    acts = [[] for _ in hs]
    for c in range(wd_ref.shape[0] // th):
        cols = slice(c * th, (c + 1) * th)
        for k, xn in enumerate(xns):
            g = _dot(xn, wg_ref[:, cols])
            u = _dot(xn, wu_ref[:, cols])
            acts[k].append((g * jax.nn.sigmoid(g) * u).astype(BF16))
    accs = [_dot(jnp.concatenate(a, axis=1), wd_ref[...]) for a in acts]
    for h, acc, rows in zip(hs, accs, subs):
        h = h + acc
        for half in range(2):
            part = slice(half * (ts // 2), (half + 1) * (ts // 2))
            out_rows = slice(rows.start + part.start, rows.start + part.stop)
            hp = h[part]
            gate = jax.nn.sigmoid(_dot(_rms(hp, gp_ref[...]).astype(BF16), wpg_ref[...]))
            proj = _dot(p_ref[0, out_rows, :].astype(BF16), wpp_ref[...])
            hp = hp + proj * gate
            if final_norm:
                hp = _rms(hp, gl_ref[...])
            o_ref[0, out_rows, :] = hp


def _ffn_ple(h, p, layer, g_ffn, g_ple, g_last, weights, attn=None, *, final_norm, tm, th, n_sub):
    B, S, D = h.shape
    w_gate, w_up, w_down, w_pg, w_pp = weights
    body = functools.partial(_ffn_ple_body, th=th, n_sub=n_sub, final_norm=final_norm,
                             with_attn=attn is not None)
    tile = lambda w: pl.BlockSpec((1, tm, w), lambda b, i: (b, i, 0))
    in_specs = [tile(D), pl.BlockSpec((None, 1, tm, p.shape[-1]), lambda b, i: (layer, b, i, 0)),
                _layer(g_ffn, layer), _resident(w_gate.shape), _resident(w_up.shape), _resident(w_down.shape),
                _layer(g_ple, layer), _resident(w_pg.shape), _resident(w_pp.shape), _resident(g_last.shape)]
    args = [h, p, g_ffn, w_gate, w_up, w_down, g_ple, w_pg, w_pp, g_last]
    if attn is not None:
        o, w_o = attn
        in_specs += [tile(o.shape[-1]), _resident(w_o.shape)]
        args += [o, w_o]
    return pl.pallas_call(
        body,
        grid=(B, S // tm),
        in_specs=in_specs,
        out_specs=tile(D),
        out_shape=jax.ShapeDtypeStruct((B, S, D), F32),
        compiler_params=_params(2),
        name="ffn_ple",
    )(*args)


def _qkv_body(h_ref, g_ref, w_ref, *refs, tm, n_sub):
    outs, xbuf = refs[:A_GROUPS], refs[A_GROUPS]
    W = A_GROUP_WIDTH
    n_slab = h_ref.shape[2] // LANES
    ts = tm // n_sub
    xns = []
    for k in range(n_sub):
        xn = _rms(h_ref[0, k * ts:(k + 1) * ts, :], g_ref[...])
        for s in range(n_slab):
            xbuf[k, s] = xn[:, s * LANES:(s + 1) * LANES]
        xns.append(xn.astype(BF16))
    for g, d in enumerate(A_DILATIONS):
        for k in range(n_sub):
            if d == 1:
                lhs = xns[k]
            else:
                lhs = jnp.concatenate(
                    [jnp.concatenate([xbuf[k, s, pl.ds(r, ts // d, stride=d), :] for s in range(n_slab)],
                                     axis=1) for r in range(d)], axis=0).astype(BF16)
            for part in range(3):
                c = part * A_GROUPS + g
                y = _dot(lhs, w_ref[:, c * W:(c + 1) * W])
                if part == 0:
                    y = y * (A_HEAD_DIM ** -0.5)
                y = y.astype(BF16)
                for r in range(d):
                    rows_in = slice(r * (ts // d), (r + 1) * (ts // d))
                    rows_out = slice(k * (ts // d), (k + 1) * (ts // d))
                    for pr in range(W // LANES):
                        lo = (3 * pr + part) * LANES
                        outs[g][0, r, rows_out, lo:lo + LANES] = y[rows_in, pr * LANES:(pr + 1) * LANES]


def _qkv_proj(h, g, layer, w, casts, *, tm, n_sub):
    B, S, D = h.shape
    W3 = 3 * A_GROUP_WIDTH
    return _call_with_casts(
        functools.partial(_qkv_body, tm=tm, n_sub=n_sub), casts,
        grid=(B, S // tm),
        in_specs=[pl.BlockSpec((1, tm, D), lambda b, i: (b, i, 0)), _layer(g, layer), _resident(w.shape)],
        out_specs=[pl.BlockSpec((1, d, tm // d, W3), lambda b, i: (b, 0, i, 0)) for d in A_DILATIONS],
        out_shape=[jax.ShapeDtypeStruct((B, d, S // d, W3), BF16) for d in A_DILATIONS],
        args=[h, g, w],
        scratch_shapes=[pltpu.VMEM((n_sub, D // LANES, tm // n_sub, LANES), F32)],
        name="qkv_proj",
    )


def _phase_major(d):
    return d % 8 == 0


def _phase_pitch(d):
    return A_SPAN // d + 8


def _attn_body(*refs):
    n_in = 2 * A_GROUPS
    bias_ref, out_ref = refs[n_in], refs[n_in + 1]
    bufs = refs[n_in + 2:]
    pv_bufs, m_bufs, den_bufs = bufs[:A_GROUPS], bufs[A_GROUPS:2 * A_GROUPS], bufs[2 * A_GROUPS:]
    first = pl.program_id(1) == 0
    pair = pl.program_id(2)
    head0 = lax.broadcasted_iota(jnp.int32, (1, LANES), 1) < A_HEAD_DIM
    col = lax.broadcasted_iota(jnp.int32, (1, 2 * A_BLOCK), 1)
    no_prev = jnp.where(jnp.logical_and(first, col < A_BLOCK), NEG, 0.0).astype(F32)
    ones = jnp.ones((2 * A_BLOCK, LANES), BF16)
    q_cols, k_cols, v_cols = (slice(part * LANES, (part + 1) * LANES) for part in range(3))

    def token_rows(buf, d, t0, n):
        if not _phase_major(d):
            return buf[pl.ds(t0, n), :]
        return jnp.concatenate([buf[pl.ds(step, d, stride=_phase_pitch(d)), :]
                                for step in range(t0 // d, (t0 + n) // d)], axis=0)

    chunk = 2 * A_BLOCK

    def combine(c):
        group_rows = lambda group_bufs: [token_rows(group_bufs[g], d, c * chunk, chunk)
                                         for g, d in enumerate(A_DILATIONS)]
        ms = group_rows(m_bufs)
        m = functools.reduce(jnp.maximum, ms)
        ws = [jnp.exp(mg - m) for mg in ms]
        num = sum(w * pv for w, pv in zip(ws, group_rows(pv_bufs)))
        den = sum(w * dn for w, dn in zip(ws, group_rows(den_bufs)))
        out_ref[0, pl.ds(c * chunk, chunk), :] = (num / den).astype(BF16)

    for g, d in sorted(enumerate(A_DILATIONS), key=lambda gd: -gd[1]):
        cur_ref, prev_ref = refs[2 * g:2 * g + 2]
        bias2 = bias_ref[g, pl.ds(2 * pair, 2)].reshape(2 * A_BLOCK, 2 * A_BLOCK)
        bias2_first = bias2 + no_prev
        for r in range(d):
            for qb in range(A_SPAN // d // A_BLOCK):
                q = cur_ref[0, r, qb * A_BLOCK:(qb + 1) * A_BLOCK, q_cols]
                if qb == 0:
                    keys = jnp.concatenate([prev_ref[0, r, :, k_cols], cur_ref[0, r, :A_BLOCK, k_cols]], axis=0)
                    vals = jnp.concatenate([prev_ref[0, r, :, v_cols], cur_ref[0, r, :A_BLOCK, v_cols]], axis=0)
                else:
                    keys = cur_ref[0, r, (qb - 1) * A_BLOCK:(qb + 1) * A_BLOCK, k_cols]
                    vals = cur_ref[0, r, (qb - 1) * A_BLOCK:(qb + 1) * A_BLOCK, v_cols]
                zero = jnp.zeros_like(q)
                q2 = jnp.concatenate([jnp.where(head0, q, zero), jnp.where(head0, zero, q)], axis=0)
                s = _dot_t(q2, keys) + (bias2_first if qb == 0 else bias2)
                m = jnp.max(s, axis=-1, keepdims=True)
                e = jnp.exp((s - m).astype(BF16))
                pvd = _dot(e, jnp.concatenate([vals, ones], axis=1))
                if d == 1:
                    rows = pl.ds(qb * A_BLOCK, A_BLOCK)
                elif _phase_major(d):
                    rows = pl.ds(r * _phase_pitch(d) + qb * A_BLOCK, A_BLOCK)
                else:
                    rows = pl.ds(qb * A_BLOCK * d + r, A_BLOCK, stride=d)
                pv_bufs[g][rows, :] = jnp.where(head0, pvd[:A_BLOCK, :LANES], pvd[A_BLOCK:, :LANES])
                den_bufs[g][rows, :] = jnp.where(head0, pvd[:A_BLOCK, LANES:], pvd[A_BLOCK:, LANES:])
                m_bufs[g][rows, :] = jnp.where(head0, m[:A_BLOCK], m[A_BLOCK:])
                if d == 1 and (qb + 1) * A_BLOCK % chunk == 0:
                    combine((qb + 1) * A_BLOCK // chunk - 1)


def _attention(qkv, bias):
    B, _, S, _ = qkv[0].shape
    n_pairs = A_GROUP_WIDTH // LANES
    in_specs, args = [], []
    for g, d in enumerate(A_DILATIONS):
        steps = A_SPAN // d
        bps = steps // A_BLOCK
        in_specs += [
            pl.BlockSpec((1, d, steps, 3 * LANES), lambda b, i, pr: (b, 0, i, pr)),
            pl.BlockSpec((1, d, A_BLOCK, 3 * LANES),
                         lambda b, i, pr, bps=bps: (b, 0, jnp.maximum(i * bps - 1, 0), pr))]
        args += [qkv[g]] * 2
    buf_rows = [d * _phase_pitch(d) if _phase_major(d) else A_SPAN for d in A_DILATIONS]
    return pl.pallas_call(
        _attn_body,
        grid=(B, S // A_SPAN, n_pairs),
        in_specs=in_specs + [_resident(bias.shape)],
        out_specs=pl.BlockSpec((1, A_SPAN, LANES), lambda b, i, pr: (b, i, pr)),
        out_shape=jax.ShapeDtypeStruct((B, S, A_GROUP_WIDTH), BF16),
        scratch_shapes=[pltpu.VMEM((rows, LANES), F32) for rows in buf_rows] * 3,
        compiler_params=_params(3),
        name="attention",
    )(*args, bias)


def _t5_bucket(n):
    nf = np.maximum(n, 1).astype(np.float32)
    large = REL_MAX_EXACT + (np.log(nf / REL_MAX_EXACT) / np.log(REL_MAX_DIST / REL_MAX_EXACT)
                             * (REL_BUCKETS - REL_MAX_EXACT)).astype(np.int32)
    large = np.minimum(large, REL_BUCKETS - 1)
    return np.where(n < REL_MAX_EXACT, n, large).astype(np.int32)


def _band_buckets():
    i = np.arange(A_BLOCK)[:, None]
    j = np.arange(2 * A_BLOCK)[None, :]
    rel = A_BLOCK + i - j
    valid = (rel >= 0) & (rel <= A_BLOCK)
    return np.stack([np.where(valid, _t5_bucket(np.where(valid, rel, 0) * d), -1)
                     for d in A_DILATIONS]).astype(np.int32)


def _bias_body(table_ref, bucket_ref, o_ref):
    g = pl.program_id(0)
    bk = bucket_ref[0]
    for hd in range(A_HEADS):
        acc = jnp.full(bk.shape, NEG, F32)
        for b in range(REL_BUCKETS):
            acc = jnp.where(bk == b, table_ref[b, g * A_HEADS + hd], acc)
        o_ref[0, hd] = acc


def _band_bias(rel_table):
    buckets = jnp.asarray(_band_buckets())
    G, Q, Kk = buckets.shape
    return pl.pallas_call(
        _bias_body,
        grid=(G,),
        in_specs=[pl.BlockSpec(memory_space=pltpu.SMEM),
                  pl.BlockSpec((1, Q, Kk), lambda g: (g, 0, 0))],
        out_specs=pl.BlockSpec((1, A_HEADS, Q, Kk), lambda g: (g, 0, 0, 0)),
        out_shape=jax.ShapeDtypeStruct((G, A_HEADS, Q, Kk), F32),
        compiler_params=_params(1),
        name="band_bias",
    )(rel_table, buckets)


def _mixer_dilated(h, g_mix, layer, w_qkv, bias, casts, *, tm, n_sub):
    qkv, conv = _qkv_proj(h, g_mix, layer, w_qkv, casts, tm=tm, n_sub=n_sub)
    return _attention(qkv, bias), conv


def _pool_body(h_ref, halo_ref, g_ref, win_ref, wgrp_ref, sc_ref, wout_ref, o_ref, *, tm, n_sub):
    i = pl.program_id(1)
    gain = g_ref[...]
    ts = tm // n_sub
    gw = h_ref.shape[2] // len(B_WINDOWS)
    xn = jnp.concatenate([_rms(halo_ref[0], gain), _rms(h_ref[0], gain)], axis=0).astype(BF16)
    ys = []
    for k in range(n_sub):
        y = _dot(xn[k * ts:(k + 1) * ts + B_HALO], win_ref[...])
        if k == 0:
            y = jnp.concatenate([jnp.where(i == 0, 0.0, y[:B_HALO]), y[B_HALO:]], axis=0)
        ys.append(y)
    pooled = []
    for k, y in enumerate(ys):
        pos = (lax.broadcasted_iota(jnp.int32, (ts, 1), 0) + (i * tm + k * ts + 1)).astype(F32)
        groups = []
        for g, win in enumerate(B_WINDOWS):
            yg = y[:, g * gw:(g + 1) * gw]
            s = yg
            sh = 1
            while sh < win:
                s = s + pltpu.roll(s, sh, axis=0)
                sh *= 2
            mean = s[B_HALO:] * (1.0 / jnp.minimum(pos, float(win)))
            groups.append((mean - yg[B_HALO:]).astype(BF16))
        pooled.append(groups)
    for k, groups in enumerate(pooled):
        z = jnp.concatenate([_dot(pg, wgrp_ref[g * gw:(g + 1) * gw, :]) for g, pg in enumerate(groups)],
                            axis=1) * sc_ref[...]
        rows = slice(k * ts, (k + 1) * ts)
        o_ref[0, rows, :] = h_ref[0, rows, :] + _dot(z.astype(BF16), wout_ref[...])


def _mixer_pool(h, g_mix, layer, weights, scale, j, casts, *, tm, n_sub):
    B, S, D = h.shape
    hpt = tm // B_HALO
    w_in, w_grp, w_out = weights
    return _call_with_casts(
        functools.partial(_pool_body, tm=tm, n_sub=n_sub), casts,
        grid=(B, S // tm),
        in_specs=[pl.BlockSpec((1, tm, D), lambda b, i: (b, i, 0)),
                  pl.BlockSpec((1, B_HALO, D), lambda b, i: (b, jnp.maximum(i * hpt - 1, 0), 0)),
                  _layer(g_mix, layer), _resident(w_in.shape), _resident(w_grp.shape), _layer(scale, j),
                  _resident(w_out.shape)],
        out_specs=pl.BlockSpec((1, tm, D), lambda b, i: (b, i, 0)),
        out_shape=jax.ShapeDtypeStruct((B, S, D), F32),
        args=[h, h, g_mix, w_in, w_grp, scale, w_out],
        name="mixer_pool",
    )


def _sgu_body(h_ref, g_ref, win_ref, vg_ref, ws_ref, bs_ref, wout_ref, o_ref, *, tm, n_sub):
    D = h_ref.shape[2]
    gw = D // C_GROUPS
    t = lax.broadcasted_iota(jnp.int32, (C_CHUNK, C_CHUNK), 0)
    s = lax.broadcasted_iota(jnp.int32, (C_CHUNK, C_CHUNK), 1)
    wms = [jnp.where(s <= t, ws_ref[g], 0.0).astype(BF16) for g in range(C_GROUPS)]
    ts = tm // n_sub
    subs = [slice(k * ts, (k + 1) * ts) for k in range(n_sub)]
    xns = [_rms(h_ref[0, rows, :], g_ref[...]).astype(BF16) for rows in subs]
    zs = [(_dot(xn, win_ref[:, D:]), _dot(xn, win_ref[:, :D])) for xn in xns]
    uvs = []
    for zv, zu in zs:
        v = jax.nn.gelu(zv)
        mu = jnp.mean(v, axis=-1, keepdims=True)
        var = jnp.mean(jnp.square(v - mu), axis=-1, keepdims=True)
        vn = ((v - mu) * lax.rsqrt(var + EPS) * vg_ref[...]).astype(BF16)
        uvs.append((jax.nn.gelu(zu), vn))
    for (u, vn), rows in zip(uvs, subs):
        cols = []
        for g in range(C_GROUPS):
            b = bs_ref[:, g:g + 1]
            cols.append(jnp.concatenate(
                [_dot(wms[g], vn[n * C_CHUNK:(n + 1) * C_CHUNK, g * gw:(g + 1) * gw]) + b
                 for n in range(ts // C_CHUNK)], axis=0))
        sp = jnp.concatenate(cols, axis=1)
        o_ref[0, rows, :] = h_ref[0, rows, :] + _dot((u * sp).astype(BF16), wout_ref[...])


def _mixer_sgu(h, g_mix, layer, weights, v_gain, w_s, b_s_t, j, casts, *, tm, n_sub):
    B, S, D = h.shape
    w_in, w_out = weights
    return _call_with_casts(
        functools.partial(_sgu_body, tm=tm, n_sub=n_sub), casts,
        grid=(B, S // tm),
        in_specs=[pl.BlockSpec((1, tm, D), lambda b, i: (b, i, 0)),
                  _layer(g_mix, layer), _resident(w_in.shape), _layer(v_gain, j),
                  _layer(w_s, j), _layer(b_s_t, j), _resident(w_out.shape)],
        out_specs=pl.BlockSpec((1, tm, D), lambda b, i: (b, i, 0)),
        out_shape=jax.ShapeDtypeStruct((B, S, D), F32),
        args=[h, g_mix, w_in, v_gain, w_s, b_s_t, w_out],
        name="mixer_sgu",
    )


def kernel(x, p, rel_table, norm_mix, norm_ffn, norm_ple, final_norm, a_w_qkv, a_w_o, b_w_in, b_w_grp, b_scale, b_w_out, c_w_in, c_v_gain, c_w_s, c_b_s, c_w_out, ffn_w_gate, ffn_w_up, ffn_w_down, ple_w_gate, ple_w_proj):
    depth = norm_mix.shape[0]
    rows = lambda v: v[..., None, :]
    tiles = dict(tm=TILE_ROWS, n_sub=SUB_TILES)
    g_mix, g_ffn, g_ple, g_last = rows(norm_mix), rows(norm_ffn), rows(norm_ple), rows(final_norm)
    b_w_grp = b_w_grp.reshape(b_w_grp.shape[0], -1, b_w_grp.shape[-1])

    def mixer_sources(i):
        kind, j = i % 3, i // 3
        stacks = {0: (a_w_qkv, a_w_o), 1: (b_w_in, b_w_grp, b_w_out), 2: (c_w_in, c_w_out)}[kind]
        return [(w, j) for w in stacks]

    ffn_sources = lambda i: [(w, i) for w in (ffn_w_gate, ffn_w_up, ffn_w_down, ple_w_gate, ple_w_proj)]
    mixer_w = [w[j].astype(BF16) for w, j in mixer_sources(0)]
    bias = _band_bias(rel_table)
    h = x
    for i in range(depth):
        kind, j = i % 3, i // 3
        casts = ffn_sources(i) + (mixer_sources(i + 1) if i + 1 < depth else [])
        attn = None
        if kind == 0:
            o, conv = _mixer_dilated(h, g_mix, i, mixer_w[0], bias, casts, **tiles)
            attn = (o, mixer_w[1])
        elif kind == 1:
            h, conv = _mixer_pool(h, g_mix, i, mixer_w, rows(b_scale), j, casts, **tiles)
        else:
            h, conv = _mixer_sgu(h, g_mix, i, mixer_w, rows(c_v_gain), c_w_s, jnp.swapaxes(c_b_s, 1, 2), j,
                                 casts, **tiles)
        ffn_w, mixer_w = conv[:5], conv[5:]
        h = _ffn_ple(h, p, i, g_ffn, g_ple, g_last, ffn_w, attn, final_norm=(i == depth - 1),
                     th=FFN_CHUNK, **tiles)
    return h
```

```python
import functools

import jax
import jax.numpy as jnp
import numpy as np
from jax import lax
from jax.experimental import pallas as pl
from jax.experimental.pallas import tpu as pltpu

EPS = 1e-6
NEG = -1e30

A_DILATIONS = (1, 4, 16)
A_GROUPS = 3
A_HEADS = 8
A_HEAD_DIM = 64
A_GROUP_WIDTH = A_HEADS * A_HEAD_DIM
A_BLOCK = 128
A_PAIRS_PER_STEP = 2
A_SPAN = A_BLOCK * max(A_DILATIONS)
REL_BUCKETS = 32
REL_MAX_EXACT = REL_BUCKETS // 2
REL_MAX_DIST = 2048
B_WINDOWS = (2, 4, 8, 16)
B_HALO = 16
C_CHUNK = 128
C_GROUPS = 4

LANES = 128
BF16_SUBLANES = 16
V7X_MXU_WIDTH = 256
V7X_VMEM_LIMIT_BYTES = 56 * 1024 * 1024

TILE_ROWS = 1024
SUB_TILES = 2
FFN_CHUNK = V7X_MXU_WIDTH

BF16 = jnp.bfloat16
F32 = jnp.float32


def _dot(a, b):
    return jnp.dot(a, b, preferred_element_type=F32)


def _dot_t(a, b):
    return lax.dot_general(a, b, (((1,), (1,)), ((), ())), preferred_element_type=F32)


def _rms(x, g):
    ms = jnp.mean(x * x, axis=-1, keepdims=True)
    return x * lax.rsqrt(ms + EPS) * g


def _resident(shape):
    nd = len(shape)
    return pl.BlockSpec(shape, lambda *_: (0,) * nd, pipeline_mode=pl.Buffered(1))


def _layer(arr, layer):
    if layer is None:
        return _resident(arr.shape)
    nd = arr.ndim - 1
    return pl.BlockSpec((None,) + arr.shape[1:], lambda *_: (layer,) + (0,) * nd,
                        pipeline_mode=pl.Buffered(1))


def _cast_plan(sources, grid):
    n_steps = int(np.prod(grid))

    def step_of(idx):
        step = 0
        for n, i in zip(grid, idx):
            step = step * n + i
        return step

    in_specs, out_specs, out_shapes = [], [], []
    for arr, layer in sources:
        _, R, C = arr.shape
        n_blk = n_steps
        while R % n_blk or (R // n_blk) % BF16_SUBLANES:
            n_blk //= 2
        rep = n_steps // n_blk
        in_specs.append(pl.BlockSpec((None, R // n_blk, C),
                                     lambda *idx, layer=layer, rep=rep: (layer, step_of(idx) // rep, 0)))
        out_specs.append(pl.BlockSpec((R // n_blk, C), lambda *idx, rep=rep: (step_of(idx) // rep, 0)))
        out_shapes.append(jax.ShapeDtypeStruct((R, C), BF16))
    return in_specs, out_specs, out_shapes


def _with_casts(body, n_in, n_out, n_cast):
    def wrapped(*refs):
        ins, casts_in = refs[:n_in], refs[n_in:n_in + n_cast]
        outs = refs[n_in + n_cast:n_in + n_cast + n_out]
        casts_out = refs[n_in + n_cast + n_out:n_in + 2 * n_cast + n_out]
        for src, dst in zip(casts_in, casts_out):
            dst[...] = src[...].astype(dst.dtype)
        body(*ins, *outs, *refs[n_in + 2 * n_cast + n_out:])
    return wrapped


def _call_with_casts(body, casts, *, grid, in_specs, out_specs, out_shape, args, name, **kw):
    single = not isinstance(out_shape, (list, tuple))
    out_specs, out_shape = ([out_specs], [out_shape]) if single else (list(out_specs), list(out_shape))
    c_in, c_out, c_shapes = _cast_plan(casts, grid)
    res = pl.pallas_call(
        _with_casts(body, len(in_specs), len(out_specs), len(casts)),
        grid=grid,
        in_specs=list(in_specs) + c_in,
        out_specs=out_specs + c_out,
        out_shape=out_shape + c_shapes,
        compiler_params=pltpu.CompilerParams(dimension_semantics=("arbitrary",) * len(grid),
                                             vmem_limit_bytes=V7X_VMEM_LIMIT_BYTES),
        name=name, **kw,
    )(*args, *[arr for arr, _ in casts])
    outs, conv = res[:len(out_specs)], list(res[len(out_specs):])
    return (outs[0] if single else outs), conv


def _params(n_grid_dims):
    return pltpu.CompilerParams(
        dimension_semantics=("parallel",) * n_grid_dims,
        vmem_limit_bytes=V7X_VMEM_LIMIT_BYTES,
    )


def _ffn_ple_body(h_ref, p_ref, gf_ref, wg_ref, wu_ref, wd_ref, gp_ref, wpg_ref, wpp_ref, gl_ref,
                  *refs, th, n_sub, final_norm, with_attn):
    o_ref = refs[-1]
    ts = h_ref.shape[1] // n_sub
    subs = [slice(k * ts, (k + 1) * ts) for k in range(n_sub)]
    hs = [h_ref[0, rows, :] for rows in subs]
    if with_attn:
        a_ref, wo_ref = refs[:2]
        hs = [h + _dot(a_ref[0, rows, :], wo_ref[...]) for h, rows in zip(hs, subs)]
    xns = [_rms(h, gf_ref[...]).astype(BF16) for h in hs]
    acts = [[] for _ in hs]
    for c in range(wd_ref.shape[0] // th):
        cols = slice(c * th, (c + 1) * th)
        for k, xn in enumerate(xns):
            g = _dot(xn, wg_ref[:, cols])
            u = _dot(xn, wu_ref[:, cols])
            acts[k].append((g * jax.nn.sigmoid(g) * u).astype(BF16))
    accs = [_dot(jnp.concatenate(a, axis=1), wd_ref[...]) for a in acts]
    for h, acc, rows in zip(hs, accs, subs):
        h = h + acc
        for half in range(2):
            part = slice(half * (ts // 2), (half + 1) * (ts // 2))
            out_rows = slice(rows.start + part.start, rows.start + part.stop)
            hp = h[part]
            gate = jax.nn.sigmoid(_dot(_rms(hp, gp_ref[...]).astype(BF16), wpg_ref[...]))
            proj = _dot(p_ref[0, out_rows, :].astype(BF16), wpp_ref[...])
            hp = hp + proj * gate
            if final_norm:
                hp = _rms(hp, gl_ref[...])
            o_ref[0, out_rows, :] = hp


def _ffn_ple(h, p, layer, g_ffn, g_ple, g_last, weights, attn=None, *, final_norm, tm, th, n_sub):
    B, S, D = h.shape
    w_gate, w_up, w_down, w_pg, w_pp = weights
    body = functools.partial(_ffn_ple_body, th=th, n_sub=n_sub, final_norm=final_norm,
                             with_attn=attn is not None)
    tile = lambda w: pl.BlockSpec((1, tm, w), lambda b, i: (b, i, 0))
    in_specs = [tile(D), pl.BlockSpec((None, 1, tm, p.shape[-1]), lambda b, i: (layer, b, i, 0)),
                _layer(g_ffn, layer), _resident(w_gate.shape), _resident(w_up.shape), _resident(w_down.shape),
                _layer(g_ple, layer), _resident(w_pg.shape), _resident(w_pp.shape), _resident(g_last.shape)]
    args = [h, p, g_ffn, w_gate, w_up, w_down, g_ple, w_pg, w_pp, g_last]
    if attn is not None:
        o, w_o = attn
        in_specs += [tile(o.shape[-1]), _resident(w_o.shape)]
        args += [o, w_o]
    return pl.pallas_call(
        body,
        grid=(B, S // tm),
        in_specs=in_specs,
        out_specs=tile(D),
        out_shape=jax.ShapeDtypeStruct((B, S, D), F32),
        compiler_params=_params(2),
        name="ffn_ple",
    )(*args)


def _qkv_body(h_ref, g_ref, w_ref, *refs, tm, n_sub):
    outs, xbuf = refs[:A_GROUPS], refs[A_GROUPS]
    W = A_GROUP_WIDTH
    n_slab = h_ref.shape[2] // LANES
    ts = tm // n_sub
    xns = []
    for k in range(n_sub):
        xn = _rms(h_ref[0, k * ts:(k + 1) * ts, :], g_ref[...])
        for s in range(n_slab):
            xbuf[k, s] = xn[:, s * LANES:(s + 1) * LANES]
        xns.append(xn.astype(BF16))
    for g, d in enumerate(A_DILATIONS):
        for k in range(n_sub):
            if d == 1:
                lhs = xns[k]
            else:
                lhs = jnp.concatenate(
                    [jnp.concatenate([xbuf[k, s, pl.ds(r, ts // d, stride=d), :] for s in range(n_slab)],
                                     axis=1) for r in range(d)], axis=0).astype(BF16)
            for part in range(3):
                c = part * A_GROUPS + g
                y = _dot(lhs, w_ref[:, c * W:(c + 1) * W])
                if part == 0:
                    y = y * (A_HEAD_DIM ** -0.5)
                y = y.astype(BF16)
                for r in range(d):
                    rows_in = slice(r * (ts // d), (r + 1) * (ts // d))
                    rows_out = slice(k * (ts // d), (k + 1) * (ts // d))
                    for pr in range(W // LANES):
                        lo = (3 * pr + part) * LANES
                        outs[g][0, r, rows_out, lo:lo + LANES] = y[rows_in, pr * LANES:(pr + 1) * LANES]


def _qkv_proj(h, g, layer, w, casts, *, tm, n_sub):
    B, S, D = h.shape
    W3 = 3 * A_GROUP_WIDTH
    return _call_with_casts(
        functools.partial(_qkv_body, tm=tm, n_sub=n_sub), casts,
        grid=(B, S // tm),
        in_specs=[pl.BlockSpec((1, tm, D), lambda b, i: (b, i, 0)), _layer(g, layer), _resident(w.shape)],
        out_specs=[pl.BlockSpec((1, d, tm // d, W3), lambda b, i: (b, 0, i, 0)) for d in A_DILATIONS],
        out_shape=[jax.ShapeDtypeStruct((B, d, S // d, W3), BF16) for d in A_DILATIONS],
        args=[h, g, w],
        scratch_shapes=[pltpu.VMEM((n_sub, D // LANES, tm // n_sub, LANES), F32)],
        name="qkv_proj",
    )


def _phase_major(d):
    return d % 8 == 0


def _phase_pitch(d):
    return A_SPAN // d + 8


def _attn_body(*refs):
    n_in = 2 * A_GROUPS
    bias_ref, out_ref = refs[n_in], refs[n_in + 1]
    bufs = refs[n_in + 2:]
    pv_bufs, m_bufs, den_bufs = bufs[:A_GROUPS], bufs[A_GROUPS:2 * A_GROUPS], bufs[2 * A_GROUPS:]
    first = pl.program_id(1) == 0
    for pp in range(A_PAIRS_PER_STEP):
        _attn_pair(refs, first, pl.program_id(2) * A_PAIRS_PER_STEP + pp, pp)


def _attn_pair(refs, first, pair, pp):
    n_in = 2 * A_GROUPS
    bias_ref, out_ref = refs[n_in], refs[n_in + 1]
    bufs = refs[n_in + 2:]
    pv_bufs, m_bufs, den_bufs = bufs[:A_GROUPS], bufs[A_GROUPS:2 * A_GROUPS], bufs[2 * A_GROUPS:]
    head0 = lax.broadcasted_iota(jnp.int32, (1, LANES), 1) < A_HEAD_DIM
    col = lax.broadcasted_iota(jnp.int32, (1, 2 * A_BLOCK), 1)
    no_prev = jnp.where(jnp.logical_and(first, col < A_BLOCK), NEG, 0.0).astype(F32)
    ones = jnp.ones((2 * A_BLOCK, LANES), BF16)
    q_cols, k_cols, v_cols = (slice((3 * pp + part) * LANES, (3 * pp + part + 1) * LANES) for part in range(3))

    def token_rows(buf, d, t0, n):
        if not _phase_major(d):
            return buf[pl.ds(t0, n), :]
        return jnp.concatenate([buf[pl.ds(step, d, stride=_phase_pitch(d)), :]
                                for step in range(t0 // d, (t0 + n) // d)], axis=0)

    chunk = 2 * A_BLOCK

    def combine(c):
        group_rows = lambda group_bufs: [token_rows(group_bufs[g], d, c * chunk, chunk)
                                         for g, d in enumerate(A_DILATIONS)]
        ms = group_rows(m_bufs)
        m = functools.reduce(jnp.maximum, ms)
        ws = [jnp.exp(mg - m) for mg in ms]
        num = sum(w * pv for w, pv in zip(ws, group_rows(pv_bufs)))
        den = sum(w * dn for w, dn in zip(ws, group_rows(den_bufs)))
        out_ref[0, pl.ds(c * chunk, chunk), pp * LANES:(pp + 1) * LANES] = (num / den).astype(BF16)

    for g, d in sorted(enumerate(A_DILATIONS), key=lambda gd: -gd[1]):
        cur_ref, prev_ref = refs[2 * g:2 * g + 2]
        bias2 = bias_ref[g, pl.ds(2 * pair, 2)].reshape(2 * A_BLOCK, 2 * A_BLOCK)
        bias2_first = bias2 + no_prev
        for r in range(d):
            for qb in range(A_SPAN // d // A_BLOCK):
                q = cur_ref[0, r, qb * A_BLOCK:(qb + 1) * A_BLOCK, q_cols]
                if qb == 0:
                    keys = jnp.concatenate([prev_ref[0, r, :, k_cols], cur_ref[0, r, :A_BLOCK, k_cols]], axis=0)
                    vals = jnp.concatenate([prev_ref[0, r, :, v_cols], cur_ref[0, r, :A_BLOCK, v_cols]], axis=0)
                else:
                    keys = cur_ref[0, r, (qb - 1) * A_BLOCK:(qb + 1) * A_BLOCK, k_cols]
                    vals = cur_ref[0, r, (qb - 1) * A_BLOCK:(qb + 1) * A_BLOCK, v_cols]
                zero = jnp.zeros_like(q)
                q2 = jnp.concatenate([jnp.where(head0, q, zero), jnp.where(head0, zero, q)], axis=0)
                s = _dot_t(q2, keys) + (bias2_first if qb == 0 else bias2)
                m = jnp.max(s, axis=-1, keepdims=True)
                e = jnp.exp((s - m).astype(BF16))
                pvd = _dot(e, jnp.concatenate([vals, ones], axis=1))
                if d == 1:
                    rows = pl.ds(qb * A_BLOCK, A_BLOCK)
                elif _phase_major(d):
                    rows = pl.ds(r * _phase_pitch(d) + qb * A_BLOCK, A_BLOCK)
                else:
                    rows = pl.ds(qb * A_BLOCK * d + r, A_BLOCK, stride=d)
                pv_bufs[g][rows, :] = jnp.where(head0, pvd[:A_BLOCK, :LANES], pvd[A_BLOCK:, :LANES])
                den_bufs[g][rows, :] = jnp.where(head0, pvd[:A_BLOCK, LANES:], pvd[A_BLOCK:, LANES:])
                m_bufs[g][rows, :] = jnp.where(head0, m[:A_BLOCK], m[A_BLOCK:])
                if d == 1 and (qb + 1) * A_BLOCK % chunk == 0:
                    combine((qb + 1) * A_BLOCK // chunk - 1)


def _attention(qkv, bias):
    B, _, S, _ = qkv[0].shape
    n_pairs = A_GROUP_WIDTH // LANES
    in_specs, args = [], []
    for g, d in enumerate(A_DILATIONS):
        steps = A_SPAN // d
        bps = steps // A_BLOCK
        in_specs += [
            pl.BlockSpec((1, d, steps, 3 * LANES * A_PAIRS_PER_STEP), lambda b, i, pr: (b, 0, i, pr)),
            pl.BlockSpec((1, d, A_BLOCK, 3 * LANES * A_PAIRS_PER_STEP),
                         lambda b, i, pr, bps=bps: (b, 0, jnp.maximum(i * bps - 1, 0), pr))]
        args += [qkv[g]] * 2
    buf_rows = [d * _phase_pitch(d) if _phase_major(d) else A_SPAN for d in A_DILATIONS]
    return pl.pallas_call(
        _attn_body,
        grid=(B, S // A_SPAN, n_pairs // A_PAIRS_PER_STEP),
        in_specs=in_specs + [_resident(bias.shape)],
        out_specs=pl.BlockSpec((1, A_SPAN, LANES * A_PAIRS_PER_STEP), lambda b, i, pr: (b, i, pr)),
        out_shape=jax.ShapeDtypeStruct((B, S, A_GROUP_WIDTH), BF16),
        scratch_shapes=[pltpu.VMEM((rows, LANES), F32) for rows in buf_rows] * 3,
        compiler_params=_params(3),
        name="attention",
    )(*args, bias)


def _t5_bucket(n):
    nf = np.maximum(n, 1).astype(np.float32)
    large = REL_MAX_EXACT + (np.log(nf / REL_MAX_EXACT) / np.log(REL_MAX_DIST / REL_MAX_EXACT)
                             * (REL_BUCKETS - REL_MAX_EXACT)).astype(np.int32)
    large = np.minimum(large, REL_BUCKETS - 1)
    return np.where(n < REL_MAX_EXACT, n, large).astype(np.int32)


def _band_buckets():
    i = np.arange(A_BLOCK)[:, None]
    j = np.arange(2 * A_BLOCK)[None, :]
    rel = A_BLOCK + i - j
    valid = (rel >= 0) & (rel <= A_BLOCK)
    return np.stack([np.where(valid, _t5_bucket(np.where(valid, rel, 0) * d), -1)
                     for d in A_DILATIONS]).astype(np.int32)


def _bias_body(table_ref, bucket_ref, o_ref):
    g = pl.program_id(0)
    bk = bucket_ref[0]
    for hd in range(A_HEADS):
        acc = jnp.full(bk.shape, NEG, F32)
        for b in range(REL_BUCKETS):
            acc = jnp.where(bk == b, table_ref[b, g * A_HEADS + hd], acc)
        o_ref[0, hd] = acc


def _band_bias(rel_table):
    buckets = jnp.asarray(_band_buckets())
    G, Q, Kk = buckets.shape
    return pl.pallas_call(
        _bias_body,
        grid=(G,),
        in_specs=[pl.BlockSpec(memory_space=pltpu.SMEM),
                  pl.BlockSpec((1, Q, Kk), lambda g: (g, 0, 0))],
        out_specs=pl.BlockSpec((1, A_HEADS, Q, Kk), lambda g: (g, 0, 0, 0)),
        out_shape=jax.ShapeDtypeStruct((G, A_HEADS, Q, Kk), F32),
        compiler_params=_params(1),
        name="band_bias",
    )(rel_table, buckets)


def _mixer_dilated(h, g_mix, layer, w_qkv, bias, casts, *, tm, n_sub):
    qkv, conv = _qkv_proj(h, g_mix, layer, w_qkv, casts, tm=tm, n_sub=n_sub)
    return _attention(qkv, bias), conv


def _pool_body(h_ref, halo_ref, g_ref, win_ref, wgrp_ref, sc_ref, wout_ref, o_ref, *, tm, n_sub):
    i = pl.program_id(1)
    gain = g_ref[...]
    ts = tm // n_sub
    gw = h_ref.shape[2] // len(B_WINDOWS)
    xn = jnp.concatenate([_rms(halo_ref[0], gain), _rms(h_ref[0], gain)], axis=0).astype(BF16)
    ys = []
    for k in range(n_sub):
        y = _dot(xn[k * ts:(k + 1) * ts + B_HALO], win_ref[...])
        if k == 0:
            y = jnp.concatenate([jnp.where(i == 0, 0.0, y[:B_HALO]), y[B_HALO:]], axis=0)
        ys.append(y)
    pooled = []
    for k, y in enumerate(ys):
        pos = (lax.broadcasted_iota(jnp.int32, (ts, 1), 0) + (i * tm + k * ts + 1)).astype(F32)
        groups = []
        for g, win in enumerate(B_WINDOWS):
            yg = y[:, g * gw:(g + 1) * gw]
            s = yg
            sh = 1
            while sh < win:
                s = s + pltpu.roll(s, sh, axis=0)
                sh *= 2
            mean = s[B_HALO:] * (1.0 / jnp.minimum(pos, float(win)))
            groups.append((mean - yg[B_HALO:]).astype(BF16))
        pooled.append(groups)
    for k, groups in enumerate(pooled):
        z = jnp.concatenate([_dot(pg, wgrp_ref[g * gw:(g + 1) * gw, :]) for g, pg in enumerate(groups)],
                            axis=1) * sc_ref[...]
        rows = slice(k * ts, (k + 1) * ts)
        o_ref[0, rows, :] = h_ref[0, rows, :] + _dot(z.astype(BF16), wout_ref[...])


def _mixer_pool(h, g_mix, layer, weights, scale, j, casts, *, tm, n_sub):
    B, S, D = h.shape
    hpt = tm // B_HALO
    w_in, w_grp, w_out = weights
    return _call_with_casts(
        functools.partial(_pool_body, tm=tm, n_sub=n_sub), casts,
        grid=(B, S // tm),
        in_specs=[pl.BlockSpec((1, tm, D), lambda b, i: (b, i, 0)),
                  pl.BlockSpec((1, B_HALO, D), lambda b, i: (b, jnp.maximum(i * hpt - 1, 0), 0)),
                  _layer(g_mix, layer), _resident(w_in.shape), _resident(w_grp.shape), _layer(scale, j),
                  _resident(w_out.shape)],
        out_specs=pl.BlockSpec((1, tm, D), lambda b, i: (b, i, 0)),
        out_shape=jax.ShapeDtypeStruct((B, S, D), F32),
        args=[h, h, g_mix, w_in, w_grp, scale, w_out],
        name="mixer_pool",
    )


def _sgu_body(h_ref, g_ref, win_ref, vg_ref, ws_ref, bs_ref, wout_ref, o_ref, *, tm, n_sub):
    D = h_ref.shape[2]
    gw = D // C_GROUPS
    t = lax.broadcasted_iota(jnp.int32, (C_CHUNK, C_CHUNK), 0)
    s = lax.broadcasted_iota(jnp.int32, (C_CHUNK, C_CHUNK), 1)
    wms = [jnp.where(s <= t, ws_ref[g], 0.0).astype(BF16) for g in range(C_GROUPS)]
    ts = tm // n_sub
    subs = [slice(k * ts, (k + 1) * ts) for k in range(n_sub)]
    xns = [_rms(h_ref[0, rows, :], g_ref[...]).astype(BF16) for rows in subs]
    zs = [(_dot(xn, win_ref[:, D:]), _dot(xn, win_ref[:, :D])) for xn in xns]
    uvs = []
    for zv, zu in zs:
        v = jax.nn.gelu(zv)
        mu = jnp.mean(v, axis=-1, keepdims=True)
        var = jnp.mean(jnp.square(v - mu), axis=-1, keepdims=True)
        vn = ((v - mu) * lax.rsqrt(var + EPS) * vg_ref[...]).astype(BF16)
        uvs.append((jax.nn.gelu(zu), vn))
    for (u, vn), rows in zip(uvs, subs):
        cols = []
        for g in range(C_GROUPS):
            b = bs_ref[:, g:g + 1]
            cols.append(jnp.concatenate(
                [_dot(wms[g], vn[n * C_CHUNK:(n + 1) * C_CHUNK, g * gw:(g + 1) * gw]) + b
                 for n in range(ts // C_CHUNK)], axis=0))
        sp = jnp.concatenate(cols, axis=1)
        o_ref[0, rows, :] = h_ref[0, rows, :] + _dot((u * sp).astype(BF16), wout_ref[...])


def _mixer_sgu(h, g_mix, layer, weights, v_gain, w_s, b_s_t, j, casts, *, tm, n_sub):
    B, S, D = h.shape
    w_in, w_out = weights
    return _call_with_casts(
        functools.partial(_sgu_body, tm=tm, n_sub=n_sub), casts,
        grid=(B, S // tm),
        in_specs=[pl.BlockSpec((1, tm, D), lambda b, i: (b, i, 0)),
                  _layer(g_mix, layer), _resident(w_in.shape), _layer(v_gain, j),
                  _layer(w_s, j), _layer(b_s_t, j), _resident(w_out.shape)],
        out_specs=pl.BlockSpec((1, tm, D), lambda b, i: (b, i, 0)),
        out_shape=jax.ShapeDtypeStruct((B, S, D), F32),
        args=[h, g_mix, w_in, v_gain, w_s, b_s_t, w_out],
        name="mixer_sgu",
    )


def kernel(x, p, rel_table, norm_mix, norm_ffn, norm_ple, final_norm, a_w_qkv, a_w_o, b_w_in, b_w_grp, b_scale, b_w_out, c_w_in, c_v_gain, c_w_s, c_b_s, c_w_out, ffn_w_gate, ffn_w_up, ffn_w_down, ple_w_gate, ple_w_proj):
    depth = norm_mix.shape[0]
    rows = lambda v: v[..., None, :]
    tiles = dict(tm=TILE_ROWS, n_sub=SUB_TILES)
    g_mix, g_ffn, g_ple, g_last = rows(norm_mix), rows(norm_ffn), rows(norm_ple), rows(final_norm)
    b_w_grp = b_w_grp.reshape(b_w_grp.shape[0], -1, b_w_grp.shape[-1])

    def mixer_sources(i):
        kind, j = i % 3, i // 3
        stacks = {0: (a_w_qkv, a_w_o), 1: (b_w_in, b_w_grp, b_w_out), 2: (c_w_in, c_w_out)}[kind]
        return [(w, j) for w in stacks]

    ffn_sources = lambda i: [(w, i) for w in (ffn_w_gate, ffn_w_up, ffn_w_down, ple_w_gate, ple_w_proj)]
    mixer_w = [w[j].astype(BF16) for w, j in mixer_sources(0)]
    bias = _band_bias(rel_table)
    h = x
    for i in range(depth):
        kind, j = i % 3, i // 3
        casts = ffn_sources(i) + (mixer_sources(i + 1) if i + 1 < depth else [])
        attn = None
        if kind == 0:
            o, conv = _mixer_dilated(h, g_mix, i, mixer_w[0], bias, casts, **tiles)
            attn = (o, mixer_w[1])
        elif kind == 1:
            h, conv = _mixer_pool(h, g_mix, i, mixer_w, rows(b_scale), j, casts, **tiles)
        else:
            h, conv = _mixer_sgu(h, g_mix, i, mixer_w, rows(c_v_gain), c_w_s, jnp.swapaxes(c_b_s, 1, 2), j,
                                 casts, **tiles)
        ffn_w, mixer_w = conv[:5], conv[5:]
        h = _ffn_ple(h, p, i, g_ffn, g_ple, g_last, ffn_w, attn, final_norm=(i == depth - 1),
                     th=FFN_CHUNK, **tiles)
    return h
```

```python
import functools

import jax
import jax.numpy as jnp
import numpy as np
from jax import lax
from jax.experimental import pallas as pl
from jax.experimental.pallas import tpu as pltpu

EPS = 1e-6
NEG = -1e30

A_DILATIONS = (1, 4, 16)
A_GROUPS = 3
A_HEADS = 8
A_HEAD_DIM = 64
A_GROUP_WIDTH = A_HEADS * A_HEAD_DIM
A_BLOCK = 128
A_PAIRS_PER_STEP = 2
A_SPAN = A_BLOCK * max(A_DILATIONS)
REL_BUCKETS = 32
REL_MAX_EXACT = REL_BUCKETS // 2
REL_MAX_DIST = 2048
B_WINDOWS = (2, 4, 8, 16)
B_HALO = 16
C_CHUNK = 128
C_GROUPS = 4

LANES = 128
BF16_SUBLANES = 16
V7X_MXU_WIDTH = 256
V7X_VMEM_LIMIT_BYTES = 56 * 1024 * 1024

TILE_ROWS = 1024
SUB_TILES = 2
FFN_CHUNK = V7X_MXU_WIDTH

BF16 = jnp.bfloat16
F32 = jnp.float32


def _dot(a, b):
    return jnp.dot(a, b, preferred_element_type=F32)


def _dot_t(a, b):
    return lax.dot_general(a, b, (((1,), (1,)), ((), ())), preferred_element_type=F32)


def _rms(x, g):
    ms = jnp.mean(x * x, axis=-1, keepdims=True)
    return x * lax.rsqrt(ms + EPS) * g


def _resident(shape):
    nd = len(shape)
    return pl.BlockSpec(shape, lambda *_: (0,) * nd, pipeline_mode=pl.Buffered(1))


def _layer(arr, layer):
    if layer is None:
        return _resident(arr.shape)
    nd = arr.ndim - 1
    return pl.BlockSpec((None,) + arr.shape[1:], lambda *_: (layer,) + (0,) * nd,
                        pipeline_mode=pl.Buffered(1))


def _cast_plan(sources, grid):
    n_steps = int(np.prod(grid))

    def step_of(idx):
        step = 0
        for n, i in zip(grid, idx):
            step = step * n + i
        return step

    in_specs, out_specs, out_shapes = [], [], []
    for arr, layer in sources:
        _, R, C = arr.shape
        n_blk = n_steps
        while R % n_blk or (R // n_blk) % BF16_SUBLANES:
            n_blk //= 2
        rep = n_steps // n_blk
        in_specs.append(pl.BlockSpec((None, R // n_blk, C),
                                     lambda *idx, layer=layer, rep=rep: (layer, step_of(idx) // rep, 0)))
        out_specs.append(pl.BlockSpec((R // n_blk, C), lambda *idx, rep=rep: (step_of(idx) // rep, 0)))
        out_shapes.append(jax.ShapeDtypeStruct((R, C), BF16))
    return in_specs, out_specs, out_shapes


def _with_casts(body, n_in, n_out, n_cast):
    def wrapped(*refs):
        ins, casts_in = refs[:n_in], refs[n_in:n_in + n_cast]
        outs = refs[n_in + n_cast:n_in + n_cast + n_out]
        casts_out = refs[n_in + n_cast + n_out:n_in + 2 * n_cast + n_out]
        for src, dst in zip(casts_in, casts_out):
            dst[...] = src[...].astype(dst.dtype)
        body(*ins, *outs, *refs[n_in + 2 * n_cast + n_out:])
    return wrapped


def _call_with_casts(body, casts, *, grid, in_specs, out_specs, out_shape, args, name, **kw):
    single = not isinstance(out_shape, (list, tuple))
    out_specs, out_shape = ([out_specs], [out_shape]) if single else (list(out_specs), list(out_shape))
    c_in, c_out, c_shapes = _cast_plan(casts, grid)
    res = pl.pallas_call(
        _with_casts(body, len(in_specs), len(out_specs), len(casts)),
        grid=grid,
        in_specs=list(in_specs) + c_in,
        out_specs=out_specs + c_out,
        out_shape=out_shape + c_shapes,
        compiler_params=pltpu.CompilerParams(dimension_semantics=("arbitrary",) * len(grid),
                                             vmem_limit_bytes=V7X_VMEM_LIMIT_BYTES),
        name=name, **kw,
    )(*args, *[arr for arr, _ in casts])
    outs, conv = res[:len(out_specs)], list(res[len(out_specs):])
    return (outs[0] if single else outs), conv


def _params(n_grid_dims):
    return pltpu.CompilerParams(
        dimension_semantics=("parallel",) * n_grid_dims,
        vmem_limit_bytes=V7X_VMEM_LIMIT_BYTES,
    )


def _ffn_ple_body(h_ref, p_ref, gf_ref, wg_ref, wu_ref, wd_ref, gp_ref, wpg_ref, wpp_ref, gl_ref,
                  *refs, th, n_sub, final_norm, with_attn):
    o_ref = refs[-1]
    ts = h_ref.shape[1] // n_sub
    subs = [slice(k * ts, (k + 1) * ts) for k in range(n_sub)]
    hs = [h_ref[0, rows, :] for rows in subs]
    if with_attn:
        a_ref, wo_ref = refs[:2]
        hs = [h + _dot(a_ref[0, rows, :], wo_ref[...]) for h, rows in zip(hs, subs)]
    xns = [_rms(h, gf_ref[...]).astype(BF16) for h in hs]
    acts = [[] for _ in hs]
    for c in range(wd_ref.shape[0] // th):
        cols = slice(c * th, (c + 1) * th)
        for k, xn in enumerate(xns):
            g = _dot(xn, wg_ref[:, cols])
            u = _dot(xn, wu_ref[:, cols])
            acts[k].append((g * jax.nn.sigmoid(g) * u).astype(BF16))
    accs = [_dot(jnp.concatenate(a, axis=1), wd_ref[...]) for a in acts]
    for h, acc, rows in zip(hs, accs, subs):
        h = h + acc
        for half in range(2):
            part = slice(half * (ts // 2), (half + 1) * (ts // 2))
            out_rows = slice(rows.start + part.start, rows.start + part.stop)
            hp = h[part]
            gate = jax.nn.sigmoid(_dot(_rms(hp, gp_ref[...]).astype(BF16), wpg_ref[...]))
            proj = _dot(p_ref[0, out_rows, :].astype(BF16), wpp_ref[...])
            hp = hp + proj * gate
            if final_norm:
                hp = _rms(hp, gl_ref[...])
            o_ref[0, out_rows, :] = hp


def _ffn_ple(h, p, layer, g_ffn, g_ple, g_last, weights, attn=None, *, final_norm, tm, th, n_sub):
    B, S, D = h.shape
    w_gate, w_up, w_down, w_pg, w_pp = weights
    body = functools.partial(_ffn_ple_body, th=th, n_sub=n_sub, final_norm=final_norm,
                             with_attn=attn is not None)
    tile = lambda w: pl.BlockSpec((1, tm, w), lambda b, i: (b, i, 0))
    in_specs = [tile(D), pl.BlockSpec((None, 1, tm, p.shape[-1]), lambda b, i: (layer, b, i, 0)),
                _layer(g_ffn, layer), _resident(w_gate.shape), _resident(w_up.shape), _resident(w_down.shape),
                _layer(g_ple, layer), _resident(w_pg.shape), _resident(w_pp.shape), _resident(g_last.shape)]
    args = [h, p, g_ffn, w_gate, w_up, w_down, g_ple, w_pg, w_pp, g_last]
    if attn is not None:
        o, w_o = attn
        in_specs += [tile(o.shape[-1]), _resident(w_o.shape)]
        args += [o, w_o]
    return pl.pallas_call(
        body,
        grid=(B, S // tm),
        in_specs=in_specs,
        out_specs=tile(D),
        out_shape=jax.ShapeDtypeStruct((B, S, D), F32),
        compiler_params=_params(2),
        name="ffn_ple",
    )(*args)


def _qkv_body(h_ref, g_ref, w_ref, *refs, tm, n_sub):
    outs, xbuf = refs[:A_GROUPS], refs[A_GROUPS]
    W = A_GROUP_WIDTH
    n_slab = h_ref.shape[2] // LANES
    ts = tm // n_sub
    xns = []
    for k in range(n_sub):
        xn = _rms(h_ref[0, k * ts:(k + 1) * ts, :], g_ref[...])
        for s in range(n_slab):
            xbuf[k, s] = xn[:, s * LANES:(s + 1) * LANES]
        xns.append(xn.astype(BF16))
    for g, d in enumerate(A_DILATIONS):
        for k in range(n_sub):
            if d == 1:
                lhs = xns[k]
            else:
                lhs = jnp.concatenate(
                    [jnp.concatenate([xbuf[k, s, pl.ds(r, ts // d, stride=d), :] for s in range(n_slab)],
                                     axis=1) for r in range(d)], axis=0).astype(BF16)
            for part in range(3):
                c = part * A_GROUPS + g
                y = _dot(lhs, w_ref[:, c * W:(c + 1) * W])
                if part == 0:
                    y = y * (A_HEAD_DIM ** -0.5)
                y = y.astype(BF16)
                for r in range(d):
                    rows_in = slice(r * (ts // d), (r + 1) * (ts // d))
                    rows_out = slice(k * (ts // d), (k + 1) * (ts // d))
                    for pr in range(W // LANES):
                        lo = (3 * pr + part) * LANES
                        outs[g][0, r, rows_out, lo:lo + LANES] = y[rows_in, pr * LANES:(pr + 1) * LANES]


def _qkv_proj(h, g, layer, w, casts, *, tm, n_sub):
    B, S, D = h.shape
    W3 = 3 * A_GROUP_WIDTH
    return _call_with_casts(
        functools.partial(_qkv_body, tm=tm, n_sub=n_sub), casts,
        grid=(B, S // tm),
        in_specs=[pl.BlockSpec((1, tm, D), lambda b, i: (b, i, 0)), _layer(g, layer), _resident(w.shape)],
        out_specs=[pl.BlockSpec((1, d, tm // d, W3), lambda b, i: (b, 0, i, 0)) for d in A_DILATIONS],
        out_shape=[jax.ShapeDtypeStruct((B, d, S // d, W3), BF16) for d in A_DILATIONS],
        args=[h, g, w],
        scratch_shapes=[pltpu.VMEM((n_sub, D // LANES, tm // n_sub, LANES), F32)],
        name="qkv_proj",
    )


def _phase_major(d):
    return d % 8 == 0


def _phase_pitch(d):
    return A_SPAN // d + 8


def _attn_body(*refs):
    n_in = 2 * A_GROUPS
    bias_ref, out_ref = refs[n_in], refs[n_in + 1]
    bufs = refs[n_in + 2:]
    pv_bufs, m_bufs, den_bufs = bufs[:A_GROUPS], bufs[A_GROUPS:2 * A_GROUPS], bufs[2 * A_GROUPS:]
    first = pl.program_id(1) == 0
    for pp in range(A_PAIRS_PER_STEP):
        _attn_pair(refs, first, pl.program_id(2) * A_PAIRS_PER_STEP + pp, pp)


def _attn_pair(refs, first, pair, pp):
    n_in = 2 * A_GROUPS
    bias_ref, out_ref = refs[n_in], refs[n_in + 1]
    bufs = refs[n_in + 2:]
    pv_bufs, m_bufs, den_bufs = bufs[:A_GROUPS], bufs[A_GROUPS:2 * A_GROUPS], bufs[2 * A_GROUPS:]
    head0 = lax.broadcasted_iota(jnp.int32, (1, LANES), 1) < A_HEAD_DIM
    col = lax.broadcasted_iota(jnp.int32, (1, 2 * A_BLOCK), 1)
    no_prev = jnp.where(jnp.logical_and(first, col < A_BLOCK), NEG, 0.0).astype(F32)
    ones = jnp.ones((2 * A_BLOCK, LANES), BF16)
    q_cols, k_cols, v_cols = (slice((3 * pp + part) * LANES, (3 * pp + part + 1) * LANES) for part in range(3))

    def token_rows(buf, d, t0, n):
        if not _phase_major(d):
            return buf[pl.ds(t0, n), :]
        return jnp.concatenate([buf[pl.ds(step, d, stride=_phase_pitch(d)), :]
                                for step in range(t0 // d, (t0 + n) // d)], axis=0)

    chunk = 2 * A_BLOCK

    def combine(c):
        group_rows = lambda group_bufs: [token_rows(group_bufs[g], d, c * chunk, chunk)
                                         for g, d in enumerate(A_DILATIONS)]
        ms = group_rows(m_bufs)
        m = functools.reduce(jnp.maximum, ms)
        ws = [jnp.exp(mg - m) for mg in ms]
        num = sum(w * pv for w, pv in zip(ws, group_rows(pv_bufs)))
        den = sum(w * dn for w, dn in zip(ws, group_rows(den_bufs)))
        out_ref[0, pl.ds(c * chunk, chunk), pp * LANES:(pp + 1) * LANES] = (num / den).astype(BF16)

    for g, d in sorted(enumerate(A_DILATIONS), key=lambda gd: -gd[1]):
        cur_ref, prev_ref = refs[2 * g:2 * g + 2]
        bias2 = bias_ref[g, pl.ds(2 * pair, 2)].reshape(2 * A_BLOCK, 2 * A_BLOCK)
        bias2_first = bias2 + no_prev
        for r in range(d):
            for qb in range(A_SPAN // d // A_BLOCK):
                q = cur_ref[0, r, qb * A_BLOCK:(qb + 1) * A_BLOCK, q_cols]
                if qb == 0:
                    keys = jnp.concatenate([prev_ref[0, r, :, k_cols], cur_ref[0, r, :A_BLOCK, k_cols]], axis=0)
                    vals = jnp.concatenate([prev_ref[0, r, :, v_cols], cur_ref[0, r, :A_BLOCK, v_cols]], axis=0)
                else:
                    keys = cur_ref[0, r, (qb - 1) * A_BLOCK:(qb + 1) * A_BLOCK, k_cols]
                    vals = cur_ref[0, r, (qb - 1) * A_BLOCK:(qb + 1) * A_BLOCK, v_cols]
                zero = jnp.zeros_like(q)
                q2 = jnp.concatenate([jnp.where(head0, q, zero), jnp.where(head0, zero, q)], axis=0)
                s = _dot_t(q2, keys) + (bias2_first if qb == 0 else bias2)
                m = jnp.max(s, axis=-1, keepdims=True)
                e = jnp.exp((s - m).astype(BF16))
                pvd = _dot(e, jnp.concatenate([vals, ones], axis=1))
                if d == 1:
                    rows = pl.ds(qb * A_BLOCK, A_BLOCK)
                elif _phase_major(d):
                    rows = pl.ds(r * _phase_pitch(d) + qb * A_BLOCK, A_BLOCK)
                else:
                    rows = pl.ds(qb * A_BLOCK * d + r, A_BLOCK, stride=d)
                pv_bufs[g][rows, :] = jnp.where(head0, pvd[:A_BLOCK, :LANES], pvd[A_BLOCK:, :LANES])
                den_bufs[g][rows, :] = jnp.where(head0, pvd[:A_BLOCK, LANES:], pvd[A_BLOCK:, LANES:])
                m_bufs[g][rows, :] = jnp.where(head0, m[:A_BLOCK], m[A_BLOCK:])
                if d == 1 and (qb + 1) * A_BLOCK % chunk == 0:
                    combine((qb + 1) * A_BLOCK // chunk - 1)


def _attention(qkv, bias):
    B, _, S, _ = qkv[0].shape
    n_pairs = A_GROUP_WIDTH // LANES
    in_specs, args = [], []
    for g, d in enumerate(A_DILATIONS):
        steps = A_SPAN // d
        bps = steps // A_BLOCK
        in_specs += [
            pl.BlockSpec((1, d, steps, 3 * LANES * A_PAIRS_PER_STEP), lambda b, i, pr: (b, 0, i, pr)),
            pl.BlockSpec((1, d, A_BLOCK, 3 * LANES * A_PAIRS_PER_STEP),
                         lambda b, i, pr, bps=bps: (b, 0, jnp.maximum(i * bps - 1, 0), pr))]
        args += [qkv[g]] * 2
    buf_rows = [d * _phase_pitch(d) if _phase_major(d) else A_SPAN for d in A_DILATIONS]
    return pl.pallas_call(
        _attn_body,
        grid=(B, S // A_SPAN, n_pairs // A_PAIRS_PER_STEP),
        in_specs=in_specs + [_resident(bias.shape)],
        out_specs=pl.BlockSpec((1, A_SPAN, LANES * A_PAIRS_PER_STEP), lambda b, i, pr: (b, i, pr)),
        out_shape=jax.ShapeDtypeStruct((B, S, A_GROUP_WIDTH), BF16),
        scratch_shapes=[pltpu.VMEM((rows, LANES), F32) for rows in buf_rows] * 3,
        compiler_params=_params(3),
        name="attention",
    )(*args, bias)


def _t5_bucket(n):
    nf = np.maximum(n, 1).astype(np.float32)
    large = REL_MAX_EXACT + (np.log(nf / REL_MAX_EXACT) / np.log(REL_MAX_DIST / REL_MAX_EXACT)
                             * (REL_BUCKETS - REL_MAX_EXACT)).astype(np.int32)
    large = np.minimum(large, REL_BUCKETS - 1)
    return np.where(n < REL_MAX_EXACT, n, large).astype(np.int32)


def _band_buckets():
    i = np.arange(A_BLOCK)[:, None]
    j = np.arange(2 * A_BLOCK)[None, :]
    rel = A_BLOCK + i - j
    valid = (rel >= 0) & (rel <= A_BLOCK)
    return np.stack([np.where(valid, _t5_bucket(np.where(valid, rel, 0) * d), -1)
                     for d in A_DILATIONS]).astype(np.int32)


def _bias_body(table_ref, bucket_ref, o_ref):
    g = pl.program_id(0)
    bk = bucket_ref[0]
    for hd in range(A_HEADS):
        acc = jnp.full(bk.shape, NEG, F32)
        for b in range(REL_BUCKETS):
            acc = jnp.where(bk == b, table_ref[b, g * A_HEADS + hd], acc)
        o_ref[0, hd] = acc


def _band_bias(rel_table):
    buckets = jnp.asarray(_band_buckets())
    G, Q, Kk = buckets.shape
    return pl.pallas_call(
        _bias_body,
        grid=(G,),
        in_specs=[pl.BlockSpec(memory_space=pltpu.SMEM),
                  pl.BlockSpec((1, Q, Kk), lambda g: (g, 0, 0))],
        out_specs=pl.BlockSpec((1, A_HEADS, Q, Kk), lambda g: (g, 0, 0, 0)),
        out_shape=jax.ShapeDtypeStruct((G, A_HEADS, Q, Kk), F32),
        compiler_params=_params(1),
        name="band_bias",
    )(rel_table, buckets)


def _mixer_dilated(h, g_mix, layer, w_qkv, bias, casts, *, tm, n_sub):
    qkv, conv = _qkv_proj(h, g_mix, layer, w_qkv, casts, tm=tm, n_sub=n_sub)
    return _attention(qkv, bias), conv


def _pool_body(h_ref, halo_ref, g_ref, win_ref, wgrp_ref, sc_ref, wout_ref, o_ref, *, tm, n_sub):
    i = pl.program_id(1)
    gain = g_ref[...]
    ts = tm // n_sub
    gw = h_ref.shape[2] // len(B_WINDOWS)
    xn = jnp.concatenate([_rms(halo_ref[0], gain), _rms(h_ref[0], gain)], axis=0).astype(BF16)
    ys = []
    for k in range(n_sub):
        y = _dot(xn[k * ts:(k + 1) * ts + B_HALO], win_ref[...])
        if k == 0:
            y = jnp.concatenate([jnp.where(i == 0, 0.0, y[:B_HALO]), y[B_HALO:]], axis=0)
        ys.append(y)
    pooled = []
    for k, y in enumerate(ys):
        pos = (lax.broadcasted_iota(jnp.int32, (ts, 1), 0) + (i * tm + k * ts + 1)).astype(F32)
        groups = []
        for g, win in enumerate(B_WINDOWS):
            yg = y[:, g * gw:(g + 1) * gw]
            s = yg
            sh = 1
            while sh < win:
                s = s + pltpu.roll(s, sh, axis=0)
                sh *= 2
            mean = s[B_HALO:] * (1.0 / jnp.minimum(pos, float(win)))
            groups.append((mean - yg[B_HALO:]).astype(BF16))
        pooled.append(groups)
    for k, groups in enumerate(pooled):
        z = jnp.concatenate([_dot(pg, wgrp_ref[g * gw:(g + 1) * gw, :]) for g, pg in enumerate(groups)],
                            axis=1) * sc_ref[...]
        rows = slice(k * ts, (k + 1) * ts)
        o_ref[0, rows, :] = h_ref[0, rows, :] + _dot(z.astype(BF16), wout_ref[...])


def _mixer_pool(h, g_mix, layer, weights, scale, j, casts, *, tm, n_sub):
    B, S, D = h.shape
    hpt = tm // B_HALO
    w_in, w_grp, w_out = weights
    return _call_with_casts(
        functools.partial(_pool_body, tm=tm, n_sub=n_sub), casts,
        grid=(B, S // tm),
        in_specs=[pl.BlockSpec((1, tm, D), lambda b, i: (b, i, 0)),
                  pl.BlockSpec((1, B_HALO, D), lambda b, i: (b, jnp.maximum(i * hpt - 1, 0), 0)),
                  _layer(g_mix, layer), _resident(w_in.shape), _resident(w_grp.shape), _layer(scale, j),
                  _resident(w_out.shape)],
        out_specs=pl.BlockSpec((1, tm, D), lambda b, i: (b, i, 0)),
        out_shape=jax.ShapeDtypeStruct((B, S, D), F32),
        args=[h, h, g_mix, w_in, w_grp, scale, w_out],
        name="mixer_pool",
    )


def _sgu_body(h_ref, g_ref, win_ref, vg_ref, ws_ref, bs_ref, wout_ref, o_ref, *, tm, n_sub):
    D = h_ref.shape[2]
    gw = D // C_GROUPS
    t = lax.broadcasted_iota(jnp.int32, (C_CHUNK, C_CHUNK), 0)
    s = lax.broadcasted_iota(jnp.int32, (C_CHUNK, C_CHUNK), 1)
    wms = [jnp.where(s <= t, ws_ref[g], 0.0).astype(BF16) for g in range(C_GROUPS)]
    ts = tm // n_sub
    subs = [slice(k * ts, (k + 1) * ts) for k in range(n_sub)]
    xns = [_rms(h_ref[0, rows, :], g_ref[...]).astype(BF16) for rows in subs]
    zs = [(_dot(xn, win_ref[:, D:]), _dot(xn, win_ref[:, :D])) for xn in xns]
    uvs = []
    for zv, zu in zs:
        v = jax.nn.gelu(zv)
        mu = jnp.mean(v, axis=-1, keepdims=True)
        var = jnp.mean(jnp.square(v - mu), axis=-1, keepdims=True)
        vn = ((v - mu) * lax.rsqrt(var + EPS) * vg_ref[...]).astype(BF16)
        uvs.append((jax.nn.gelu(zu), vn))
    for (u, vn), rows in zip(uvs, subs):
        cols = []
        for g in range(C_GROUPS):
            b = bs_ref[:, g:g + 1]
            cols.append(jnp.concatenate(
                [_dot(wms[g], vn[n * C_CHUNK:(n + 1) * C_CHUNK, g * gw:(g + 1) * gw]) + b
                 for n in range(ts // C_CHUNK)], axis=0))
        sp = jnp.concatenate(cols, axis=1)
        o_ref[0, rows, :] = h_ref[0, rows, :] + _dot((u * sp).astype(BF16), wout_ref[...])


def _mixer_sgu(h, g_mix, layer, weights, v_gain, w_s, b_s_t, j, casts, *, tm, n_sub):
    B, S, D = h.shape
    w_in, w_out = weights
    return _call_with_casts(
        functools.partial(_sgu_body, tm=tm, n_sub=n_sub), casts,
        grid=(B, S // tm),
        in_specs=[pl.BlockSpec((1, tm, D), lambda b, i: (b, i, 0)),
                  _layer(g_mix, layer), _resident(w_in.shape), _layer(v_gain, j),
                  _layer(w_s, j), _layer(b_s_t, j), _resident(w_out.shape)],
        out_specs=pl.BlockSpec((1, tm, D), lambda b, i: (b, i, 0)),
        out_shape=jax.ShapeDtypeStruct((B, S, D), F32),
        args=[h, g_mix, w_in, v_gain, w_s, b_s_t, w_out],
        name="mixer_sgu",
    )


def kernel(x, p, rel_table, norm_mix, norm_ffn, norm_ple, final_norm, a_w_qkv, a_w_o, b_w_in, b_w_grp, b_scale, b_w_out, c_w_in, c_v_gain, c_w_s, c_b_s, c_w_out, ffn_w_gate, ffn_w_up, ffn_w_down, ple_w_gate, ple_w_proj):
    depth = norm_mix.shape[0]
    rows = lambda v: v[..., None, :]
    tiles = dict(tm=TILE_ROWS, n_sub=SUB_TILES)
    g_mix, g_ffn, g_ple, g_last = rows(norm_mix), rows(norm_ffn), rows(norm_ple), rows(final_norm)
    b_w_grp = b_w_grp.reshape(b_w_grp.shape[0], -1, b_w_grp.shape[-1])

    def mixer_sources(i):
        kind, j = i % 3, i // 3
        stacks = {0: (a_w_qkv, a_w_o), 1: (b_w_in, b_w_grp, b_w_out), 2: (c_w_in, c_w_out)}[kind]
        return [(w, j) for w in stacks]

    ffn_sources = lambda i: [(w, i) for w in (ffn_w_gate, ffn_w_up, ffn_w_down, ple_w_gate, ple_w_proj)]
    mixer_w = [w[j].astype(BF16) for w, j in mixer_sources(0)]
    bias = _band_bias(rel_table)
    h = x
    for i in range(depth):
        kind, j = i % 3, i // 3
        casts = ffn_sources(i) + (mixer_sources(i + 1) if i + 1 < depth else [])
        attn = None
        if kind == 0:
            o, conv = _mixer_dilated(h, g_mix, i, mixer_w[0], bias, casts, **tiles)
            attn = (o, mixer_w[1])
        elif kind == 1:
            h, conv = _mixer_pool(h, g_mix, i, mixer_w, rows(b_scale), j, casts,
                                  tm=2 * TILE_ROWS, n_sub=2 * SUB_TILES)
        else:
            h, conv = _mixer_sgu(h, g_mix, i, mixer_w, rows(c_v_gain), c_w_s, jnp.swapaxes(c_b_s, 1, 2), j,
                                 casts, **tiles)
        ffn_w, mixer_w = conv[:5], conv[5:]
        h = _ffn_ple(h, p, i, g_ffn, g_ple, g_last, ffn_w, attn, final_norm=(i == depth - 1),
                     th=FFN_CHUNK, **tiles)
    return h
```

```python
import functools

import jax
import jax.numpy as jnp
import numpy as np
from jax import lax
from jax.experimental import pallas as pl
from jax.experimental.pallas import tpu as pltpu

EPS = 1e-6
NEG = -1e30

A_DILATIONS = (1, 4, 16)
A_GROUPS = 3
A_HEADS = 8
A_HEAD_DIM = 64
A_GROUP_WIDTH = A_HEADS * A_HEAD_DIM
A_BLOCK = 128
A_PAIRS_PER_STEP = 2
A_SPAN = A_BLOCK * max(A_DILATIONS)
REL_BUCKETS = 32
REL_MAX_EXACT = REL_BUCKETS // 2
REL_MAX_DIST = 2048
B_WINDOWS = (2, 4, 8, 16)
B_HALO = 16
C_CHUNK = 128
C_GROUPS = 4

LANES = 128
BF16_SUBLANES = 16
V7X_MXU_WIDTH = 256
V7X_VMEM_LIMIT_BYTES = 56 * 1024 * 1024

TILE_ROWS = 1024
SUB_TILES = 2
FFN_CHUNK = V7X_MXU_WIDTH

BF16 = jnp.bfloat16
F32 = jnp.float32


def _dot(a, b):
    return jnp.dot(a, b, preferred_element_type=F32)


def _dot_t(a, b):
    return lax.dot_general(a, b, (((1,), (1,)), ((), ())), preferred_element_type=F32)


def _rms(x, g):
    ms = jnp.mean(x * x, axis=-1, keepdims=True)
    return x * lax.rsqrt(ms + EPS) * g


def _resident(shape):
    nd = len(shape)
    return pl.BlockSpec(shape, lambda *_: (0,) * nd, pipeline_mode=pl.Buffered(1))


def _layer(arr, layer):
    if layer is None:
        return _resident(arr.shape)
    nd = arr.ndim - 1
    return pl.BlockSpec((None,) + arr.shape[1:], lambda *_: (layer,) + (0,) * nd,
                        pipeline_mode=pl.Buffered(1))


def _cast_plan(sources, grid):
    n_steps = int(np.prod(grid))

    def step_of(idx):
        step = 0
        for n, i in zip(grid, idx):
            step = step * n + i
        return step

    in_specs, out_specs, out_shapes = [], [], []
    for arr, layer in sources:
        _, R, C = arr.shape
        n_blk = n_steps
        while R % n_blk or (R // n_blk) % BF16_SUBLANES:
            n_blk //= 2
        rep = n_steps // n_blk
        in_specs.append(pl.BlockSpec((None, R // n_blk, C),
                                     lambda *idx, layer=layer, rep=rep: (layer, step_of(idx) // rep, 0)))
        out_specs.append(pl.BlockSpec((R // n_blk, C), lambda *idx, rep=rep: (step_of(idx) // rep, 0)))
        out_shapes.append(jax.ShapeDtypeStruct((R, C), BF16))
    return in_specs, out_specs, out_shapes


def _with_casts(body, n_in, n_out, n_cast):
    def wrapped(*refs):
        ins, casts_in = refs[:n_in], refs[n_in:n_in + n_cast]
        outs = refs[n_in + n_cast:n_in + n_cast + n_out]
        casts_out = refs[n_in + n_cast + n_out:n_in + 2 * n_cast + n_out]
        for src, dst in zip(casts_in, casts_out):
            dst[...] = src[...].astype(dst.dtype)
        body(*ins, *outs, *refs[n_in + 2 * n_cast + n_out:])
    return wrapped


def _call_with_casts(body, casts, *, grid, in_specs, out_specs, out_shape, args, name, **kw):
    single = not isinstance(out_shape, (list, tuple))
    out_specs, out_shape = ([out_specs], [out_shape]) if single else (list(out_specs), list(out_shape))
    c_in, c_out, c_shapes = _cast_plan(casts, grid)
    res = pl.pallas_call(
        _with_casts(body, len(in_specs), len(out_specs), len(casts)),
        grid=grid,
        in_specs=list(in_specs) + c_in,
        out_specs=out_specs + c_out,
        out_shape=out_shape + c_shapes,
        compiler_params=pltpu.CompilerParams(dimension_semantics=("arbitrary",) * len(grid),
                                             vmem_limit_bytes=V7X_VMEM_LIMIT_BYTES),
        name=name, **kw,
    )(*args, *[arr for arr, _ in casts])
    outs, conv = res[:len(out_specs)], list(res[len(out_specs):])
    return (outs[0] if single else outs), conv


def _params(n_grid_dims):
    return pltpu.CompilerParams(
        dimension_semantics=("parallel",) * n_grid_dims,
        vmem_limit_bytes=V7X_VMEM_LIMIT_BYTES,
    )


def _ffn_ple_body(h_ref, p_ref, gf_ref, wg_ref, wu_ref, wd_ref, gp_ref, wpg_ref, wpp_ref, gl_ref,
                  *refs, th, n_sub, final_norm, with_attn):
    o_ref = refs[-1]
    ts = h_ref.shape[1] // n_sub
    subs = [slice(k * ts, (k + 1) * ts) for k in range(n_sub)]
    hs = [h_ref[0, rows, :] for rows in subs]
    if with_attn:
        a_ref, wo_ref = refs[:2]
        hs = [h + _dot(a_ref[0, rows, :], wo_ref[...]) for h, rows in zip(hs, subs)]
    xns = [_rms(h, gf_ref[...]).astype(BF16) for h in hs]
    acts = [[] for _ in hs]
    for c in range(wd_ref.shape[0] // th):
        cols = slice(c * th, (c + 1) * th)
        for k, xn in enumerate(xns):
            g = _dot(xn, wg_ref[:, cols])
            u = _dot(xn, wu_ref[:, cols])
            acts[k].append((g * jax.nn.sigmoid(g) * u).astype(BF16))
    accs = [_dot(jnp.concatenate(a, axis=1), wd_ref[...]) for a in acts]
    for h, acc, rows in zip(hs, accs, subs):
        h = h + acc
        for half in range(2):
            part = slice(half * (ts // 2), (half + 1) * (ts // 2))
            out_rows = slice(rows.start + part.start, rows.start + part.stop)
            hp = h[part]
            gate = jax.nn.sigmoid(_dot(_rms(hp, gp_ref[...]).astype(BF16), wpg_ref[...]))
            proj = _dot(p_ref[0, out_rows, :].astype(BF16), wpp_ref[...])
            hp = hp + proj * gate
            if final_norm:
                hp = _rms(hp, gl_ref[...])
            o_ref[0, out_rows, :] = hp


def _ffn_ple(h, p, layer, g_ffn, g_ple, g_last, weights, attn=None, *, final_norm, tm, th, n_sub):
    B, S, D = h.shape
    w_gate, w_up, w_down, w_pg, w_pp = weights
    body = functools.partial(_ffn_ple_body, th=th, n_sub=n_sub, final_norm=final_norm,
                             with_attn=attn is not None)
    tile = lambda w: pl.BlockSpec((1, tm, w), lambda b, i: (b, i, 0))
    in_specs = [tile(D), pl.BlockSpec((None, 1, tm, p.shape[-1]), lambda b, i: (layer, b, i, 0)),
                _layer(g_ffn, layer), _resident(w_gate.shape), _resident(w_up.shape), _resident(w_down.shape),
                _layer(g_ple, layer), _resident(w_pg.shape), _resident(w_pp.shape), _resident(g_last.shape)]
    args = [h, p, g_ffn, w_gate, w_up, w_down, g_ple, w_pg, w_pp, g_last]
    if attn is not None:
        o, w_o = attn
        in_specs += [tile(o.shape[-1]), _resident(w_o.shape)]
        args += [o, w_o]
    return pl.pallas_call(
        body,
        grid=(B, S // tm),
        in_specs=in_specs,
        out_specs=tile(D),
        out_shape=jax.ShapeDtypeStruct((B, S, D), F32),
        compiler_params=_params(2),
        name="ffn_ple",
    )(*args)


def _qkv_body(h_ref, g_ref, w_ref, *refs, tm, n_sub):
    outs, xbuf = refs[:A_GROUPS], refs[A_GROUPS]
    W = A_GROUP_WIDTH
    n_slab = h_ref.shape[2] // LANES
    ts = tm // n_sub
    xns = []
    for k in range(n_sub):
        xn = _rms(h_ref[0, k * ts:(k + 1) * ts, :], g_ref[...])
        for s in range(n_slab):
            xbuf[k, s] = xn[:, s * LANES:(s + 1) * LANES]
        xns.append(xn.astype(BF16))
    for g, d in enumerate(A_DILATIONS):
        for k in range(n_sub):
            if d == 1:
                lhs = xns[k]
            else:
                lhs = jnp.concatenate(
                    [jnp.concatenate([xbuf[k, s, pl.ds(r, ts // d, stride=d), :] for s in range(n_slab)],
                                     axis=1) for r in range(d)], axis=0).astype(BF16)
            for part in range(3):
                c = part * A_GROUPS + g
                y = _dot(lhs, w_ref[:, c * W:(c + 1) * W])
                if part == 0:
                    y = y * (A_HEAD_DIM ** -0.5)
                y = y.astype(BF16)
                for r in range(d):
                    rows_in = slice(r * (ts // d), (r + 1) * (ts // d))
                    rows_out = slice(k * (ts // d), (k + 1) * (ts // d))
                    for pr in range(W // LANES):
                        lo = (3 * pr + part) * LANES
                        outs[g][0, r, rows_out, lo:lo + LANES] = y[rows_in, pr * LANES:(pr + 1) * LANES]


def _qkv_proj(h, g, layer, w, casts, *, tm, n_sub):
    B, S, D = h.shape
    W3 = 3 * A_GROUP_WIDTH
    return _call_with_casts(
        functools.partial(_qkv_body, tm=tm, n_sub=n_sub), casts,
        grid=(B, S // tm),
        in_specs=[pl.BlockSpec((1, tm, D), lambda b, i: (b, i, 0)), _layer(g, layer), _resident(w.shape)],
        out_specs=[pl.BlockSpec((1, d, tm // d, W3), lambda b, i: (b, 0, i, 0)) for d in A_DILATIONS],
        out_shape=[jax.ShapeDtypeStruct((B, d, S // d, W3), BF16) for d in A_DILATIONS],
        args=[h, g, w],
        scratch_shapes=[pltpu.VMEM((n_sub, D // LANES, tm // n_sub, LANES), F32)],
        name="qkv_proj",
    )


def _phase_major(d):
    return d % 8 == 0


def _phase_pitch(d):
    return A_SPAN // d + 8


def _attn_body(*refs):
    first = pl.program_id(1) == 0
    for pp in range(A_PAIRS_PER_STEP):
        _attn_pair(refs, first, pl.program_id(2) * A_PAIRS_PER_STEP + pp, pp)


def _attn_pair(refs, first, pair, pp):
    n_in = 2 * A_GROUPS
    bias_ref, out_ref = refs[n_in], refs[n_in + 1]
    bufs = refs[n_in + 2:]
    pv_bufs, m_bufs, den_bufs = bufs[:A_GROUPS], bufs[A_GROUPS:2 * A_GROUPS], bufs[2 * A_GROUPS:]
    head0 = lax.broadcasted_iota(jnp.int32, (1, LANES), 1) < A_HEAD_DIM
    col = lax.broadcasted_iota(jnp.int32, (1, 2 * A_BLOCK), 1)
    no_prev = jnp.where(jnp.logical_and(first, col < A_BLOCK), NEG, 0.0).astype(F32)
    ones = jnp.ones((2 * A_BLOCK, LANES), BF16)
    q_cols, k_cols, v_cols = (slice((3 * pp + part) * LANES, (3 * pp + part + 1) * LANES) for part in range(3))

    def token_rows(buf, d, t0, n):
        if not _phase_major(d):
            return buf[pl.ds(t0, n), :]
        return jnp.concatenate([buf[pl.ds(step, d, stride=_phase_pitch(d)), :]
                                for step in range(t0 // d, (t0 + n) // d)], axis=0)

    chunk = 2 * A_BLOCK

    def combine(c):
        group_rows = lambda group_bufs: [token_rows(group_bufs[g], d, c * chunk, chunk)
                                         for g, d in enumerate(A_DILATIONS)]
        ms = group_rows(m_bufs)
        m = functools.reduce(jnp.maximum, ms)
        ws = [jnp.exp(mg - m) for mg in ms]
        num = sum(w * pv for w, pv in zip(ws, group_rows(pv_bufs)))
        den = sum(w * dn for w, dn in zip(ws, group_rows(den_bufs)))
        out_ref[0, pl.ds(c * chunk, chunk), pp * LANES:(pp + 1) * LANES] = (num / den).astype(BF16)

    for g, d in sorted(enumerate(A_DILATIONS), key=lambda gd: -gd[1]):
        cur_ref, prev_ref = refs[2 * g:2 * g + 2]
        bias2 = bias_ref[g, pl.ds(2 * pair, 2)].reshape(2 * A_BLOCK, 2 * A_BLOCK)
        bias2_first = bias2 + no_prev
        for r in range(d):
            for qb in range(A_SPAN // d // A_BLOCK):
                q = cur_ref[0, r, qb * A_BLOCK:(qb + 1) * A_BLOCK, q_cols]
                if qb == 0:
                    keys = jnp.concatenate([prev_ref[0, r, :, k_cols], cur_ref[0, r, :A_BLOCK, k_cols]], axis=0)
                    vals = jnp.concatenate([prev_ref[0, r, :, v_cols], cur_ref[0, r, :A_BLOCK, v_cols]], axis=0)
                else:
                    keys = cur_ref[0, r, (qb - 1) * A_BLOCK:(qb + 1) * A_BLOCK, k_cols]
                    vals = cur_ref[0, r, (qb - 1) * A_BLOCK:(qb + 1) * A_BLOCK, v_cols]
                zero = jnp.zeros_like(q)
                q2 = jnp.concatenate([jnp.where(head0, q, zero), jnp.where(head0, zero, q)], axis=0)
                s = _dot_t(q2, keys) + (bias2_first if qb == 0 else bias2)
                m = jnp.max(s, axis=-1, keepdims=True)
                e = jnp.exp((s - m).astype(BF16))
                pvd = _dot(e, jnp.concatenate([vals, ones], axis=1))
                if d == 1:
                    rows = pl.ds(qb * A_BLOCK, A_BLOCK)
                elif _phase_major(d):
                    rows = pl.ds(r * _phase_pitch(d) + qb * A_BLOCK, A_BLOCK)
                else:
                    rows = pl.ds(qb * A_BLOCK * d + r, A_BLOCK, stride=d)
                pv_bufs[g][rows, :] = jnp.where(head0, pvd[:A_BLOCK, :LANES], pvd[A_BLOCK:, :LANES])
                den_bufs[g][rows, :] = jnp.where(head0, pvd[:A_BLOCK, LANES:], pvd[A_BLOCK:, LANES:])
                m_bufs[g][rows, :] = jnp.where(head0, m[:A_BLOCK], m[A_BLOCK:])
                if d == 1 and (qb + 1) * A_BLOCK % chunk == 0:
                    combine((qb + 1) * A_BLOCK // chunk - 1)


def _attention(qkv, bias):
    B, _, S, _ = qkv[0].shape
    n_pairs = A_GROUP_WIDTH // LANES
    in_specs, args = [], []
    for g, d in enumerate(A_DILATIONS):
        steps = A_SPAN // d
        bps = steps // A_BLOCK
        in_specs += [
            pl.BlockSpec((1, d, steps, 3 * LANES * A_PAIRS_PER_STEP), lambda b, i, pr: (b, 0, i, pr)),
            pl.BlockSpec((1, d, A_BLOCK, 3 * LANES * A_PAIRS_PER_STEP),
                         lambda b, i, pr, bps=bps: (b, 0, jnp.maximum(i * bps - 1, 0), pr))]
        args += [qkv[g]] * 2
    buf_rows = [d * _phase_pitch(d) if _phase_major(d) else A_SPAN for d in A_DILATIONS]
    return pl.pallas_call(
        _attn_body,
        grid=(B, S // A_SPAN, n_pairs // A_PAIRS_PER_STEP),
        in_specs=in_specs + [_resident(bias.shape)],
        out_specs=pl.BlockSpec((1, A_SPAN, LANES * A_PAIRS_PER_STEP), lambda b, i, pr: (b, i, pr)),
        out_shape=jax.ShapeDtypeStruct((B, S, A_GROUP_WIDTH), BF16),
        scratch_shapes=[pltpu.VMEM((rows, LANES), F32) for rows in buf_rows] * 3,
        compiler_params=_params(3),
        name="attention",
    )(*args, bias)


def _t5_bucket(n):
    nf = np.maximum(n, 1).astype(np.float32)
    large = REL_MAX_EXACT + (np.log(nf / REL_MAX_EXACT) / np.log(REL_MAX_DIST / REL_MAX_EXACT)
                             * (REL_BUCKETS - REL_MAX_EXACT)).astype(np.int32)
    large = np.minimum(large, REL_BUCKETS - 1)
    return np.where(n < REL_MAX_EXACT, n, large).astype(np.int32)


def _band_buckets():
    i = np.arange(A_BLOCK)[:, None]
    j = np.arange(2 * A_BLOCK)[None, :]
    rel = A_BLOCK + i - j
    valid = (rel >= 0) & (rel <= A_BLOCK)
    return np.stack([np.where(valid, _t5_bucket(np.where(valid, rel, 0) * d), -1)
                     for d in A_DILATIONS]).astype(np.int32)


def _bias_body(table_ref, bucket_ref, o_ref):
    g = pl.program_id(0)
    bk = bucket_ref[0]
    for hd in range(A_HEADS):
        acc = jnp.full(bk.shape, NEG, F32)
        for b in range(REL_BUCKETS):
            acc = jnp.where(bk == b, table_ref[b, g * A_HEADS + hd], acc)
        o_ref[0, hd] = acc


def _band_bias(rel_table):
    buckets = jnp.asarray(_band_buckets())
    G, Q, Kk = buckets.shape
    return pl.pallas_call(
        _bias_body,
        grid=(G,),
        in_specs=[pl.BlockSpec(memory_space=pltpu.SMEM),
                  pl.BlockSpec((1, Q, Kk), lambda g: (g, 0, 0))],
        out_specs=pl.BlockSpec((1, A_HEADS, Q, Kk), lambda g: (g, 0, 0, 0)),
        out_shape=jax.ShapeDtypeStruct((G, A_HEADS, Q, Kk), F32),
        compiler_params=_params(1),
        name="band_bias",
    )(rel_table, buckets)


def _mixer_dilated(h, g_mix, layer, w_qkv, bias, casts, *, tm, n_sub):
    qkv, conv = _qkv_proj(h, g_mix, layer, w_qkv, casts, tm=tm, n_sub=n_sub)
    return _attention(qkv, bias), conv


def _pool_body(h_ref, halo_ref, g_ref, win_ref, wgrp_ref, sc_ref, wout_ref, o_ref, *, tm, n_sub):
    i = pl.program_id(1)
    gain = g_ref[...]
    ts = tm // n_sub
    gw = h_ref.shape[2] // len(B_WINDOWS)
    xn = jnp.concatenate([_rms(halo_ref[0], gain), _rms(h_ref[0], gain)], axis=0).astype(BF16)
    ys = []
    for k in range(n_sub):
        y = _dot(xn[k * ts:(k + 1) * ts + B_HALO], win_ref[...])
        if k == 0:
            y = jnp.concatenate([jnp.where(i == 0, 0.0, y[:B_HALO]), y[B_HALO:]], axis=0)
        ys.append(y)
    pooled = []
    for k, y in enumerate(ys):
        pos = (lax.broadcasted_iota(jnp.int32, (ts, 1), 0) + (i * tm + k * ts + 1)).astype(F32)
        groups = []
        for g, win in enumerate(B_WINDOWS):
            yg = y[:, g * gw:(g + 1) * gw]
            s = yg
            sh = 1
            while sh < win:
                s = s + pltpu.roll(s, sh, axis=0)
                sh *= 2
            mean = s[B_HALO:] * (1.0 / jnp.minimum(pos, float(win)))
            groups.append((mean - yg[B_HALO:]).astype(BF16))
        pooled.append(groups)
    for k, groups in enumerate(pooled):
        z = jnp.concatenate([_dot(pg, wgrp_ref[g * gw:(g + 1) * gw, :]) for g, pg in enumerate(groups)],
                            axis=1) * sc_ref[...]
        rows = slice(k * ts, (k + 1) * ts)
        o_ref[0, rows, :] = h_ref[0, rows, :] + _dot(z.astype(BF16), wout_ref[...])


def _mixer_pool(h, g_mix, layer, weights, scale, j, casts, *, tm, n_sub):
    B, S, D = h.shape
    hpt = tm // B_HALO
    w_in, w_grp, w_out = weights
    return _call_with_casts(
        functools.partial(_pool_body, tm=tm, n_sub=n_sub), casts,
        grid=(B, S // tm),
        in_specs=[pl.BlockSpec((1, tm, D), lambda b, i: (b, i, 0)),
                  pl.BlockSpec((1, B_HALO, D), lambda b, i: (b, jnp.maximum(i * hpt - 1, 0), 0)),
                  _layer(g_mix, layer), _resident(w_in.shape), _resident(w_grp.shape), _layer(scale, j),
                  _resident(w_out.shape)],
        out_specs=pl.BlockSpec((1, tm, D), lambda b, i: (b, i, 0)),
        out_shape=jax.ShapeDtypeStruct((B, S, D), F32),
        args=[h, h, g_mix, w_in, w_grp, scale, w_out],
        name="mixer_pool",
    )


def _sgu_body(h_ref, g_ref, win_ref, vg_ref, ws_ref, bs_ref, wout_ref, o_ref, *, tm, n_sub):
    D = h_ref.shape[2]
    gw = D // C_GROUPS
    t = lax.broadcasted_iota(jnp.int32, (C_CHUNK, C_CHUNK), 0)
    s = lax.broadcasted_iota(jnp.int32, (C_CHUNK, C_CHUNK), 1)
    wms = [jnp.where(s <= t, ws_ref[g], 0.0).astype(BF16) for g in range(C_GROUPS)]
    ts = tm // n_sub
    subs = [slice(k * ts, (k + 1) * ts) for k in range(n_sub)]
    xns = [_rms(h_ref[0, rows, :], g_ref[...]).astype(BF16) for rows in subs]
    zs = [(_dot(xn, win_ref[:, D:]), _dot(xn, win_ref[:, :D])) for xn in xns]
    uvs = []
    for zv, zu in zs:
        v = jax.nn.gelu(zv)
        mu = jnp.mean(v, axis=-1, keepdims=True)
        var = jnp.mean(jnp.square(v - mu), axis=-1, keepdims=True)
        vn = ((v - mu) * lax.rsqrt(var + EPS) * vg_ref[...]).astype(BF16)
        uvs.append((jax.nn.gelu(zu), vn))
    for (u, vn), rows in zip(uvs, subs):
        cols = []
        for g in range(C_GROUPS):
            b = bs_ref[:, g:g + 1]
            cols.append(jnp.concatenate(
                [_dot(wms[g], vn[n * C_CHUNK:(n + 1) * C_CHUNK, g * gw:(g + 1) * gw]) + b
                 for n in range(ts // C_CHUNK)], axis=0))
        sp = jnp.concatenate(cols, axis=1)
        o_ref[0, rows, :] = h_ref[0, rows, :] + _dot((u * sp).astype(BF16), wout_ref[...])


def _mixer_sgu(h, g_mix, layer, weights, v_gain, w_s, b_s_t, j, casts, *, tm, n_sub):
    B, S, D = h.shape
    w_in, w_out = weights
    return _call_with_casts(
        functools.partial(_sgu_body, tm=tm, n_sub=n_sub), casts,
        grid=(B, S // tm),
        in_specs=[pl.BlockSpec((1, tm, D), lambda b, i: (b, i, 0)),
                  _layer(g_mix, layer), _resident(w_in.shape), _layer(v_gain, j),
                  _layer(w_s, j), _layer(b_s_t, j), _resident(w_out.shape)],
        out_specs=pl.BlockSpec((1, tm, D), lambda b, i: (b, i, 0)),
        out_shape=jax.ShapeDtypeStruct((B, S, D), F32),
        args=[h, g_mix, w_in, v_gain, w_s, b_s_t, w_out],
        name="mixer_sgu",
    )


def kernel(x, p, rel_table, norm_mix, norm_ffn, norm_ple, final_norm, a_w_qkv, a_w_o, b_w_in, b_w_grp, b_scale, b_w_out, c_w_in, c_v_gain, c_w_s, c_b_s, c_w_out, ffn_w_gate, ffn_w_up, ffn_w_down, ple_w_gate, ple_w_proj):
    depth = norm_mix.shape[0]
    rows = lambda v: v[..., None, :]
    tiles = dict(tm=TILE_ROWS, n_sub=SUB_TILES)
    g_mix, g_ffn, g_ple, g_last = rows(norm_mix), rows(norm_ffn), rows(norm_ple), rows(final_norm)
    b_w_grp = b_w_grp.reshape(b_w_grp.shape[0], -1, b_w_grp.shape[-1])

    def mixer_sources(i):
        kind, j = i % 3, i // 3
        stacks = {0: (a_w_qkv, a_w_o), 1: (b_w_in, b_w_grp, b_w_out), 2: (c_w_in, c_w_out)}[kind]
        return [(w, j) for w in stacks]

    ffn_sources = lambda i: [(w, i) for w in (ffn_w_gate, ffn_w_up, ffn_w_down, ple_w_gate, ple_w_proj)]
    mixer_w = [w[j].astype(BF16) for w, j in mixer_sources(0)]
    bias = _band_bias(rel_table)
    h = x
    for i in range(depth):
        kind, j = i % 3, i // 3
        casts = ffn_sources(i) + (mixer_sources(i + 1) if i + 1 < depth else [])
        attn = None
        if kind == 0:
            o, conv = _mixer_dilated(h, g_mix, i, mixer_w[0], bias, casts, **tiles)
            attn = (o, mixer_w[1])
        elif kind == 1:
            h, conv = _mixer_pool(h, g_mix, i, mixer_w, rows(b_scale), j, casts,
                                  tm=2 * TILE_ROWS, n_sub=2 * SUB_TILES)
        else:
            h, conv = _mixer_sgu(h, g_mix, i, mixer_w, rows(c_v_gain), c_w_s, jnp.swapaxes(c_b_s, 1, 2), j,
                                 casts, **tiles)
        ffn_w, mixer_w = conv[:5], conv[5:]
        h = _ffn_ple(h, p, i, g_ffn, g_ple, g_last, ffn_w, attn, final_norm=(i == depth - 1),
                     th=FFN_CHUNK, **tiles)
    return h
```
